```python
import math
import jax, jax.numpy as jnp
from jax import lax
import numpy as np

D_MODEL = 1024
BATCH = 16
SEQ = 2048
DEPTH = 4
DEC_BATCH = 16
DEC_SEQ = 4096
PAST_LEN = 128

MIX_WIDTH = 2 * D_MODEL
W_A = MIX_WIDTH // 2
W_B = MIX_WIDTH - W_A
CHUNK = 128
A_HEADS = 4
A_HEAD_DIM = W_A // A_HEADS
B_GROUP_CH = 16
B_GROUPS = W_B // B_GROUP_CH
B_STATE = 64
N_DIR = 2
PROJ_COLS = 3 * W_A + 2 * W_B
EPS = 1e-6
DT_MIN = 1e-3
DT_MAX = 1e-1

kernel_name = "hymba_gmlp_s5_bidir_encoder"


def rmsnorm(x, g):
    xf = x.astype(jnp.float32)
    y = xf * lax.rsqrt(jnp.mean(xf * xf, axis=-1, keepdims=True) + EPS)
    return (y * g.astype(jnp.float32)).astype(x.dtype)


def layernorm(x, g, b):
    xf = x.astype(jnp.float32)
    mu = jnp.mean(xf, axis=-1, keepdims=True)
    xc = xf - mu
    y = xc * lax.rsqrt(jnp.mean(xc * xc, axis=-1, keepdims=True) + EPS)
    return (y * g.astype(jnp.float32) + b.astype(jnp.float32)).astype(x.dtype)


def gmlp_branch(u, v, z, ln_g, ln_b, w_s, b_s):
    n, l, _ = u.shape
    v = layernorm(v, ln_g, ln_b)
    vh = v.reshape(n, l // CHUNK, CHUNK, A_HEADS, A_HEAD_DIM)
    sv = jnp.einsum('hpq,ncqhd->ncphd', w_s, vh) + b_s.T[None, None, :, :, None]
    return u * sv.reshape(n, l, W_A) * jax.nn.silu(z)


def _cmul_combine(e1, e2):
    a1r, a1i, b1r, b1i = e1
    a2r, a2i, b2r, b2i = e2
    ar = a2r * a1r - a2i * a1i
    ai = a2r * a1i + a2i * a1r
    br = a2r * b1r - a2i * b1i + b2r
    bi = a2r * b1i + a2i * b1r + b2i
    return (ar, ai, br, bi)


def s5_direction(xg, lam_re, lam_im, log_dt, b_re, b_im, c_re, c_im):
    n, l, g, c = xg.shape
    lam_re = lam_re.astype(jnp.float32)
    lam_im = lam_im.astype(jnp.float32)
    dt = jnp.exp(log_dt.astype(jnp.float32))[:, None]
    mag = jnp.exp(lam_re * dt)
    ar = mag * jnp.cos(lam_im * dt)
    ai = mag * jnp.sin(lam_im * dt)
    nr, ni = ar - 1.0, ai
    den = lam_re * lam_re + lam_im * lam_im
    fr = (nr * lam_re + ni * lam_im) / den
    fi = (ni * lam_re - nr * lam_im) / den
    b_re = b_re.astype(jnp.float32)
    b_im = b_im.astype(jnp.float32)
    bb_re = fr[..., None] * b_re - fi[..., None] * b_im
    bb_im = fr[..., None] * b_im + fi[..., None] * b_re
    c_re = c_re.astype(jnp.float32)
    c_im = c_im.astype(jnp.float32)
    t = jnp.arange(1, CHUNK + 1, dtype=jnp.float32)[:, None, None]
    pmag = jnp.exp(t * lam_re * dt)
    pr = pmag * jnp.cos(t * lam_im * dt)
    pi = pmag * jnp.sin(t * lam_im * dt)
    xc = xg.reshape(n, l // CHUNK, CHUNK, g, c).transpose(1, 0, 2, 3, 4)

    def step(h, xk):
        hr, hi = h
        bu_r = jnp.einsum('ntgc,gpc->ntgp', xk, bb_re)
        bu_i = jnp.einsum('ntgc,gpc->ntgp', xk, bb_im)
        a_r = jnp.broadcast_to(ar, bu_r.shape)
        a_i = jnp.broadcast_to(ai, bu_i.shape)
        _, _, sr, si = lax.associative_scan(_cmul_combine, (a_r, a_i, bu_r, bu_i), axis=1)
        sr = sr + pr * hr[:, None] - pi * hi[:, None]
        si = si + pr * hi[:, None] + pi * hr[:, None]
        y = jnp.einsum('ntgp,gcp->ntgc', sr, c_re) - jnp.einsum('ntgp,gcp->ntgc', si, c_im)
        return (sr[:, -1], si[:, -1]), y

    h0 = (jnp.zeros((n, g, B_STATE), jnp.float32), jnp.zeros((n, g, B_STATE), jnp.float32))
    _, y = lax.scan(step, h0, xc)
    return y.transpose(1, 0, 2, 3, 4).reshape(n, l, g, c)


def s5_branch(xb, z, lam_re, lam_im, log_dt, b_re, b_im, c_re, c_im, d_skip, w_glu, b_glu):
    n, l, _ = xb.shape
    xg = xb.astype(jnp.float32).reshape(n, l, B_GROUPS, B_GROUP_CH)
    y_fwd = s5_direction(xg, lam_re[0], lam_im[0], log_dt[0], b_re[0], b_im[0], c_re[0], c_im[0])
    y_bwd = jnp.flip(s5_direction(jnp.flip(xg, axis=1), lam_re[1], lam_im[1], log_dt[1],
                                  b_re[1], b_im[1], c_re[1], c_im[1]), axis=1)
    y = ((y_fwd + y_bwd).reshape(n, l, W_B)).astype(xb.dtype) + d_skip * xb
    gy = jax.nn.gelu(y)
    gy = gy * jax.nn.sigmoid(gy @ w_glu + b_glu)
    return gy * jax.nn.silu(z)


def trunk(x, norm_g, w_in, ln_g, ln_b, w_s, b_s, lam_re, lam_im, log_dt,
          b_re, b_im, c_re, c_im, d_skip, w_glu, b_glu, w_out, final_g):
    splits = [W_A, 2 * W_A, 3 * W_A, 3 * W_A + W_B]
    for i in range(DEPTH):
        h = rmsnorm(x, norm_g[i])
        proj = h @ w_in[i]
        u_a, v_a, z_a, x_b, z_b = jnp.split(proj, splits, axis=-1)
        y_a = gmlp_branch(jax.nn.gelu(u_a), jax.nn.gelu(v_a), z_a, ln_g[i], ln_b[i], w_s[i], b_s[i])
        y_b = s5_branch(x_b, z_b, lam_re[i], lam_im[i], log_dt[i], b_re[i], b_im[i],
                        c_re[i], c_im[i], d_skip[i], w_glu[i], b_glu[i])
        x = x + jnp.concatenate([y_a, y_b], axis=-1) @ w_out[i]
    return rmsnorm(x, final_g)


def setup_inputs(seed: int = 0) -> dict:
    key = jax.random.key(seed)
    ks = jax.random.split(key, 24)
    f32 = jnp.float32
    x_prompt = jax.random.normal(ks[0], (BATCH, SEQ, D_MODEL), f32)
    x_sample = jax.random.normal(ks[1], (DEC_BATCH, DEC_SEQ, D_MODEL), f32)
    norm_g = 1.0 + 0.05 * jax.random.normal(ks[2], (DEPTH, D_MODEL), f32)
    w_in = jax.random.normal(ks[3], (DEPTH, D_MODEL, PROJ_COLS), f32) * D_MODEL ** -0.5
    ln_g = 1.0 + 0.05 * jax.random.normal(ks[4], (DEPTH, W_A), f32)
    ln_b = 0.02 * jax.random.normal(ks[5], (DEPTH, W_A), f32)
    w_s = jax.random.normal(ks[6], (DEPTH, A_HEADS, CHUNK, CHUNK), f32) * CHUNK ** -0.5
    b_s = 1.0 + 0.1 * jax.random.normal(ks[7], (DEPTH, A_HEADS, CHUNK), f32)
    shp = (DEPTH, N_DIR, B_GROUPS, B_STATE)
    lam_re = -0.5 + 0.01 * jax.random.normal(ks[8], shp, f32)
    lam_im = math.pi * jnp.arange(B_STATE, dtype=f32) + 0.01 * jax.random.normal(ks[9], shp, f32)
    log_dt = jax.random.uniform(ks[10], (DEPTH, N_DIR, B_GROUPS), f32,
                                math.log(DT_MIN), math.log(DT_MAX))
    bshp = (DEPTH, N_DIR, B_GROUPS, B_STATE, B_GROUP_CH)
    b_re = jax.random.normal(ks[11], bshp, f32) * (2 * B_GROUP_CH) ** -0.5
    b_im = jax.random.normal(ks[12], bshp, f32) * (2 * B_GROUP_CH) ** -0.5
    cshp = (DEPTH, N_DIR, B_GROUPS, B_GROUP_CH, B_STATE)
    c_re = jax.random.normal(ks[13], cshp, f32) * (2 * B_STATE) ** -0.5
    c_im = jax.random.normal(ks[14], cshp, f32) * (2 * B_STATE) ** -0.5
    d_skip = jax.random.normal(ks[15], (DEPTH, W_B), f32)
    w_glu = jax.random.normal(ks[16], (DEPTH, W_B, W_B), f32) * W_B ** -0.5
    b_glu = 0.02 * jax.random.normal(ks[17], (DEPTH, W_B), f32)
    w_out = jax.random.normal(ks[18], (DEPTH, MIX_WIDTH, D_MODEL), f32) * MIX_WIDTH ** -0.5
    final_g = 1.0 + 0.05 * jax.random.normal(ks[19], (D_MODEL,), f32)
    return {"x_prompt": x_prompt, "x_sample": x_sample, "norm_g": norm_g, "w_in": w_in,
            "ln_g": ln_g, "ln_b": ln_b, "w_s": w_s, "b_s": b_s,
            "lam_re": lam_re, "lam_im": lam_im, "log_dt": log_dt,
            "b_re": b_re, "b_im": b_im, "c_re": c_re, "c_im": c_im,
            "d_skip": d_skip, "w_glu": w_glu, "b_glu": b_glu, "w_out": w_out,
            "final_g": final_g}


def reference(x_prompt, x_sample, norm_g, w_in, ln_g, ln_b, w_s, b_s, lam_re, lam_im, log_dt,
              b_re, b_im, c_re, c_im, d_skip, w_glu, b_glu, w_out, final_g):
    y_prompt = trunk(x_prompt, norm_g, w_in, ln_g, ln_b, w_s, b_s, lam_re, lam_im, log_dt,
                     b_re, b_im, c_re, c_im, d_skip, w_glu, b_glu, w_out, final_g)
    y_sample = trunk(x_sample, norm_g, w_in, ln_g, ln_b, w_s, b_s, lam_re, lam_im, log_dt,
                     b_re, b_im, c_re, c_im, d_skip, w_glu, b_glu, w_out, final_g)
    return (y_prompt, y_sample)
```

```python
import functools

import jax
import jax.numpy as jnp
from jax import lax
from jax.experimental import pallas as pl
from jax.experimental.pallas import tpu as pltpu

D_MODEL = 1024
WIDTH = 1024
CHUNK = 128
HEADS = 4
HEAD_DIM = WIDTH // HEADS
GROUP_CH = 16
GROUPS = WIDTH // GROUP_CH
STATE = 64
N_DIR = 2
EPS = 1e-6

LANES = 128
DIRSTATE = N_DIR * STATE
TOEP = GROUP_CH * CHUNK
VMEM_LIMIT_BYTES = 56 * 1024 * 1024

MXU_DTYPE = jnp.bfloat16
F32 = jnp.float32


def _const_spec(shape):
    zeros = (0,) * len(shape)
    return pl.BlockSpec(shape, lambda *_: zeros, pipeline_mode=pl.Buffered(1))


def _params(n_grid_axes):
    return pltpu.CompilerParams(
        dimension_semantics=("arbitrary",) * n_grid_axes,
        vmem_limit_bytes=VMEM_LIMIT_BYTES)


def _powers(lam_re, lam_im, dt, k):
    mag = jnp.exp(k * lam_re * dt)
    ang = k * lam_im * dt
    return mag * jnp.cos(ang), mag * jnp.sin(ang)


def _gen_kernel(lamr_c, lami_c, ldt_c, lamr_r, lami_r, ldt_r, crep_r, crep_i, brep_r, brep_i,
                bt_r, bt_i, ct_r, ct_i, wcat_ref, ws_ref, a_ref, lag_scr):
    lr, li, dt_r = lamr_r[0], lami_r[0], jnp.exp(ldt_r[0])
    lrc, lic, dt_c = lamr_c[0], lami_c[0], jnp.exp(ldt_c[0])

    abar_r, abar_i = _powers(lr, li, dt_r, 1.0)
    num_r, num_i = abar_r - 1.0, abar_i
    den = lr * lr + li * li
    f_r = (num_r * lr + num_i * li) / den
    f_i = (num_i * lr - num_r * li) / den

    a_r, a_i = _powers(lr, li, dt_r, float(CHUNK))
    a_ref[0, 0:1, :] = a_r
    a_ref[0, 1:2, :] = a_i

    bb_r = f_r * brep_r[0] - f_i * brep_i[0]
    bb_i = f_r * brep_i[0] + f_i * brep_r[0]
    cb_r = crep_r[0] * bb_r - crep_i[0] * bb_i
    cb_i = crep_r[0] * bb_i + crep_i[0] * bb_r
    lag_lhs = jnp.concatenate([cb_r, -cb_i], axis=1)
    lane = lax.broadcasted_iota(jnp.int32, (DIRSTATE, 2 * CHUNK), 1)
    row_bwd = lax.broadcasted_iota(jnp.int32, (DIRSTATE, 2 * CHUNK), 0) >= STATE
    m = lane & (CHUNK - 1)
    hi = lane >= CHUNK
    k_lag = jnp.where(row_bwd, jnp.where(hi, CHUNK - m, 0), m).astype(F32)
    valid = (row_bwd & (hi | (m == 0))) | (~row_bwd & ~hi)
    p_r, p_i = _powers(lrc, lic, dt_c, k_lag)
    lag_rhs = jnp.concatenate([jnp.where(valid, p_r, 0.0), jnp.where(valid, p_i, 0.0)], axis=0)
    lag_scr[...] = jnp.dot(lag_lhs, lag_rhs, precision=lax.Precision.HIGHEST,
                           preferred_element_type=F32).reshape(lag_scr.shape)

    s_idx = lax.broadcasted_iota(jnp.int32, (CHUNK, CHUNK), 0)
    m_idx = lax.broadcasted_iota(jnp.int32, (CHUNK, CHUNK), 1)
    use_fwd = s_idx + m_idx <= CHUNK - 1

    def toeplitz_rows(c_in, carry):
        row0 = pl.multiple_of(c_in * CHUNK, CHUNK)
        for c_out in range(GROUP_CH):
            fwd = lag_scr[c_in, c_out:c_out + 1, 0:CHUNK]
            bwd = lag_scr[c_in, c_out:c_out + 1, CHUNK:2 * CHUNK]
            pre = jnp.where(use_fwd, jnp.broadcast_to(fwd, (CHUNK, CHUNK)),
                            jnp.broadcast_to(bwd, (CHUNK, CHUNK)))
            blk = pltpu.roll(pre, 0, 1, stride=1, stride_axis=0)
            wcat_ref[0, pl.ds(row0, CHUNK), c_out * CHUNK:(c_out + 1) * CHUNK] = (
                blk.astype(wcat_ref.dtype))
        return carry

    lax.fori_loop(0, GROUP_CH, toeplitz_rows, 0)

    s_pos = lax.broadcasted_iota(jnp.int32, (CHUNK, DIRSTATE), 0)
    lane_bwd = lax.broadcasted_iota(jnp.int32, (CHUNK, DIRSTATE), 1) >= STATE
    k_in = jnp.where(lane_bwd, s_pos, CHUNK - 1 - s_pos).astype(F32)
    pin_r, pin_i = _powers(lr, li, dt_r, k_in)
    bbt_r = f_r * bt_r[0] - f_i * bt_i[0]
    bbt_i = f_r * bt_i[0] + f_i * bt_r[0]
    for c_in in range(GROUP_CH):
        b_r, b_i = bbt_r[c_in:c_in + 1, :], bbt_i[c_in:c_in + 1, :]
        rows = slice(c_in * CHUNK, (c_in + 1) * CHUNK)
        ws_ref[0, rows, 0:DIRSTATE] = (pin_r * b_r - pin_i * b_i).astype(ws_ref.dtype)
        ws_ref[0, rows, DIRSTATE:2 * DIRSTATE] = (pin_r * b_i + pin_i * b_r).astype(ws_ref.dtype)

    t_pos = lax.broadcasted_iota(jnp.int32, (DIRSTATE, CHUNK), 1)
    sub_bwd = lax.broadcasted_iota(jnp.int32, (DIRSTATE, CHUNK), 0) >= STATE
    k_out = jnp.where(sub_bwd, CHUNK - t_pos, t_pos + 1).astype(F32)
    pout_r, pout_i = _powers(lrc, lic, dt_c, k_out)
    c_r_all, c_i_all = ct_r[0], ct_i[0]
    for c_out in range(GROUP_CH):
        c_r, c_i = c_r_all[:, c_out:c_out + 1], c_i_all[:, c_out:c_out + 1]
        cols = slice(c_out * CHUNK, (c_out + 1) * CHUNK)
        wcat_ref[0, TOEP:TOEP + DIRSTATE, cols] = (c_r * pout_r - c_i * pout_i).astype(wcat_ref.dtype)
        wcat_ref[0, TOEP + DIRSTATE:TOEP + 2 * DIRSTATE, cols] = (
            -(c_r * pout_i + c_i * pout_r)).astype(wcat_ref.dtype)


def _s5_operators(lam_re, lam_im, log_dt, b_re, b_im, c_re, c_im):
    depth = lam_re.shape[0]
    lg = depth * GROUPS

    def dirstate(a):
        return a.transpose(0, 2, 1, 3).reshape(lg, DIRSTATE)

    lamr, lami = dirstate(lam_re), dirstate(lam_im)
    ldt = dirstate(jnp.broadcast_to(log_dt[..., None], lam_re.shape))

    def b_layout(b):
        return b.transpose(0, 2, 4, 1, 3).reshape(depth, GROUPS, GROUP_CH, DIRSTATE)

    def c_layout(c):
        return c.transpose(0, 2, 3, 1, 4).reshape(depth, GROUPS, GROUP_CH, DIRSTATE)

    pair = (depth, GROUPS, GROUP_CH, GROUP_CH, DIRSTATE)

    def b_rep(b):
        return jnp.broadcast_to(b_layout(b)[:, :, :, None, :], pair).reshape(lg, GROUP_CH ** 2, DIRSTATE)

    def c_rep(c):
        return jnp.broadcast_to(c_layout(c)[:, :, None, :, :], pair).reshape(lg, GROUP_CH ** 2, DIRSTATE)

    def c_cols(c):
        return c.transpose(0, 2, 1, 4, 3).reshape(lg, DIRSTATE, GROUP_CH)

    args = (lamr[:, :, None], lami[:, :, None], ldt[:, :, None],
            lamr[:, None, :], lami[:, None, :], ldt[:, None, :],
            c_rep(c_re), c_rep(c_im), b_rep(b_re), b_rep(b_im),
            b_layout(b_re).reshape(lg, GROUP_CH, DIRSTATE), b_layout(b_im).reshape(lg, GROUP_CH, DIRSTATE),
            c_cols(c_re), c_cols(c_im))

    def spec(a):
        return pl.BlockSpec((1,) + a.shape[1:], lambda i: (i, 0, 0))

    return pl.pallas_call(
        _gen_kernel,
        grid=(lg,),
        in_specs=[spec(a) for a in args],
        out_specs=[pl.BlockSpec((1, TOEP + 2 * DIRSTATE, TOEP), lambda i: (i, 0, 0)),
                   pl.BlockSpec((1, TOEP, 2 * DIRSTATE), lambda i: (i, 0, 0)),
                   pl.BlockSpec((1, 2, DIRSTATE), lambda i: (i, 0, 0))],
        out_shape=[jax.ShapeDtypeStruct((lg, TOEP + 2 * DIRSTATE, TOEP), MXU_DTYPE),
                   jax.ShapeDtypeStruct((lg, TOEP, 2 * DIRSTATE), MXU_DTYPE),
                   jax.ShapeDtypeStruct((lg, 2, DIRSTATE), F32)],
        scratch_shapes=[pltpu.VMEM((GROUP_CH, GROUP_CH, 2 * CHUNK), F32)],
        compiler_params=_params(1),
        name="s5_operators",
    )(*args)


def _mix_in_kernel(x_ref, ng_ref, win_ref, lng_ref, lnb_ref, wst_ref, bs_ref,
                   ya_ref, xb_ref, xbm_ref, sz_ref):
    x = x_ref[...]
    inv = lax.rsqrt(jnp.mean(x * x, axis=0, keepdims=True) + EPS)
    h = (x * inv * ng_ref[...]).astype(MXU_DTYPE)

    def proj(k):
        return jnp.dot(win_ref[k * WIDTH:(k + 1) * WIDTH, :], h, preferred_element_type=F32)

    v = jax.nn.gelu(proj(1))
    mu = jnp.mean(v, axis=0, keepdims=True)
    vc = v - mu
    var = jnp.mean(vc * vc, axis=0, keepdims=True)
    vn = (vc * lax.rsqrt(var + EPS) * lng_ref[...] + lnb_ref[...]).astype(MXU_DTYPE)

    n_chunks = x.shape[1] // CHUNK
    heads = []
    for hd in range(HEADS):
        vh = vn[hd * HEAD_DIM:(hd + 1) * HEAD_DIM, :]
        stacked = jnp.concatenate(
            [vh[:, k * CHUNK:(k + 1) * CHUNK] for k in range(n_chunks)], axis=0)
        mixed = jnp.dot(stacked, wst_ref[hd], preferred_element_type=F32) + bs_ref[hd]
        heads.append(jnp.concatenate(
            [mixed[k * HEAD_DIM:(k + 1) * HEAD_DIM, :] for k in range(n_chunks)], axis=1))
    sv = jnp.concatenate(heads, axis=0)

    u = jax.nn.gelu(proj(0))
    ya_ref[...] = (u * sv * jax.nn.silu(proj(2))).astype(ya_ref.dtype)
    xb = proj(3)
    xb_ref[...] = xb
    xbm_ref[...] = xb.astype(xbm_ref.dtype)
    sz_ref[...] = jax.nn.silu(proj(4))


def _mix_in(x_t, ng, win_t, lng, lnb, wst, bs, tm):
    d, t = x_t.shape
    tile = lambda rows: pl.BlockSpec((rows, tm), lambda i: (0, i))
    return pl.pallas_call(
        _mix_in_kernel,
        grid=(t // tm,),
        in_specs=[tile(d), _const_spec(ng.shape), _const_spec(win_t.shape), _const_spec(lng.shape),
                  _const_spec(lnb.shape), _const_spec(wst.shape), _const_spec(bs.shape)],
        out_specs=[tile(WIDTH)] * 4,
        out_shape=[jax.ShapeDtypeStruct((WIDTH, t), MXU_DTYPE),
                   jax.ShapeDtypeStruct((WIDTH, t), F32),
                   jax.ShapeDtypeStruct((WIDTH, t), MXU_DTYPE),
                   jax.ShapeDtypeStruct((WIDTH, t), F32)],
        compiler_params=_params(1),
        name="mix_in",
    )(x_t, ng, win_t, lng, lnb, wst, bs)


def _chunk_scan(s_re, s_im, a_r, a_i, n_chunks, n_seq, is_fwd):
    h_r = jnp.zeros((n_seq, DIRSTATE), F32)
    h_i = jnp.zeros((n_seq, DIRSTATE), F32)
    fwd_r, fwd_i = [h_r] * n_chunks, [h_i] * n_chunks
    bwd_r, bwd_i = [h_r] * n_chunks, [h_i] * n_chunks
    rows = lambda a, j: a[j * n_seq:(j + 1) * n_seq, :]
    for i in range(n_chunks - 1):
        jf, jb = i, n_chunks - 1 - i
        in_r = jnp.where(is_fwd, rows(s_re, jf), rows(s_re, jb))
        in_i = jnp.where(is_fwd, rows(s_im, jf), rows(s_im, jb))
        h_r, h_i = a_r * h_r - a_i * h_i + in_r, a_r * h_i + a_i * h_r + in_i
        fwd_r[jf + 1], fwd_i[jf + 1] = h_r, h_i
        bwd_r[jb - 1], bwd_i[jb - 1] = h_r, h_i
    out_r = jnp.concatenate([jnp.where(is_fwd, f, b) for f, b in zip(fwd_r, bwd_r)], axis=0)
    out_i = jnp.concatenate([jnp.where(is_fwd, f, b) for f, b in zip(fwd_i, bwd_i)], axis=0)
    return out_r, out_i


def _s5_kernel(x_ref, wcat_ref, ws_ref, a_ref, o_ref, *, segments):
    lhs = jnp.concatenate([x_ref[c] for c in range(GROUP_CH)], axis=1)
    s = jnp.dot(lhs, ws_ref[0], preferred_element_type=F32)
    a_r, a_i = a_ref[0, 0:1, :], a_ref[0, 1:2, :]
    is_fwd = lax.broadcasted_iota(jnp.int32, (1, DIRSTATE), 1) < STATE
    parts_r, parts_i = [], []
    for row0, n_chunks, n_seq in segments:
        seg = s[row0:row0 + n_chunks * n_seq, :]
        h_r, h_i = _chunk_scan(seg[:, :DIRSTATE], seg[:, DIRSTATE:], a_r, a_i, n_chunks, n_seq, is_fwd)
        parts_r.append(h_r)
        parts_i.append(h_i)
    h = jnp.concatenate([jnp.concatenate(parts_r, axis=0), jnp.concatenate(parts_i, axis=0)], axis=1)
    lhs2 = jnp.concatenate([lhs, h.astype(lhs.dtype)], axis=1)
    for pair in range(GROUP_CH // 2):
        y = jnp.dot(lhs2, wcat_ref[0, :, pair * 2 * CHUNK:(pair + 1) * 2 * CHUNK],
                    preferred_element_type=F32)
        o_ref[2 * pair] = y[:, :CHUNK]
        o_ref[2 * pair + 1] = y[:, CHUNK:]


def _s5_mix(xb_m, wcat, ws, a_pow, layer, segments):
    t = xb_m.shape[1]
    rows = t // CHUNK
    x3 = xb_m.reshape(WIDTH, rows, CHUNK)
    group = lambda i: (layer * GROUPS + i, 0, 0)
    y3 = pl.pallas_call(
        functools.partial(_s5_kernel, segments=segments),
        grid=(GROUPS,),
        in_specs=[pl.BlockSpec((GROUP_CH, rows, CHUNK), lambda i: (i, 0, 0)),
                  pl.BlockSpec((1,) + wcat.shape[1:], group),
                  pl.BlockSpec((1,) + ws.shape[1:], group),
                  pl.BlockSpec((1,) + a_pow.shape[1:], group)],
        out_specs=pl.BlockSpec((GROUP_CH, rows, CHUNK), lambda i: (i, 0, 0)),
        out_shape=jax.ShapeDtypeStruct((WIDTH, rows, CHUNK), F32),
        compiler_params=_params(1),
        name="s5_mix",
    )(x3, wcat, ws, a_pow)
    return y3.reshape(WIDTH, t)


def _mix_out_kernel(x_ref, ya_ref, xb_ref, ys_ref, sz_ref, dsk_ref, wg_ref, bg_ref, wo_ref, fg_ref,
                    o_ref, *, final_norm):
    y = ys_ref[...] + dsk_ref[...] * xb_ref[...]
    gy = jax.nn.gelu(y)
    gate = jax.nn.sigmoid(
        jnp.dot(wg_ref[...], gy.astype(MXU_DTYPE), preferred_element_type=F32) + bg_ref[...])
    yb = (gy * gate * sz_ref[...]).astype(MXU_DTYPE)
    out = (x_ref[...]
           + jnp.dot(wo_ref[:, :WIDTH], ya_ref[...], preferred_element_type=F32)
           + jnp.dot(wo_ref[:, WIDTH:], yb, preferred_element_type=F32))
    if final_norm:
        out = out * lax.rsqrt(jnp.mean(out * out, axis=0, keepdims=True) + EPS) * fg_ref[...]
    o_ref[...] = out


def _mix_out(x_t, ya, xb, ys, sz, dsk, wg_t, bg, wo_t, fg, tm, final_norm):
    d, t = x_t.shape
    tile = lambda rows: pl.BlockSpec((rows, tm), lambda i: (0, i))
    return pl.pallas_call(
        functools.partial(_mix_out_kernel, final_norm=final_norm),
        grid=(t // tm,),
        in_specs=[tile(d), tile(WIDTH), tile(WIDTH), tile(WIDTH), tile(WIDTH),
                  _const_spec(dsk.shape), _const_spec(wg_t.shape), _const_spec(bg.shape),
                  _const_spec(wo_t.shape), _const_spec(fg.shape)],
        out_specs=tile(d),
        out_shape=jax.ShapeDtypeStruct((d, t), F32),
        compiler_params=_params(1),
        name="mix_out",
    )(x_t, ya, xb, ys, sz, dsk, wg_t, bg, wo_t, fg)


def _to_chunk_major(x):
    n, l, d = x.shape
    return x.reshape(n, l // CHUNK, CHUNK, d).transpose(3, 1, 0, 2).reshape(d, l * n)


def _from_chunk_major(x_t, n, l):
    d = x_t.shape[0]
    return x_t.reshape(d, l // CHUNK, n, CHUNK).transpose(2, 1, 3, 0).reshape(n, l, d)


def _token_tile(t):
    for tm in (512, 256, 128):
        if t % tm == 0:
            return tm
    raise ValueError(f"token count {t} is not a multiple of {CHUNK}")


def kernel(x_prompt, x_sample, norm_g, w_in, ln_g, ln_b, w_s, b_s, lam_re, lam_im, log_dt,
           b_re, b_im, c_re, c_im, d_skip, w_glu, b_glu, w_out, final_g):
    depth = norm_g.shape[0]
    inputs = (x_prompt, x_sample)
    for x in inputs:
        assert x.shape[1] % CHUNK == 0 and x.shape[2] == D_MODEL
    x_t = jnp.concatenate([_to_chunk_major(x) for x in inputs], axis=1)
    t = x_t.shape[1]
    tm = _token_tile(t)

    segments, row0 = [], 0
    for x in inputs:
        n_seq, n_chunks = x.shape[0], x.shape[1] // CHUNK
        segments.append((row0, n_chunks, n_seq))
        row0 += n_chunks * n_seq
    segments = tuple(segments)

    wcat, ws, a_pow = _s5_operators(lam_re, lam_im, log_dt, b_re, b_im, c_re, c_im)

    col = lambda a: a[:, None]
    for i in range(depth):
        ya, xb, xb_m, sz = _mix_in(
            x_t, col(norm_g[i]), w_in[i].T.astype(MXU_DTYPE), col(ln_g[i]), col(ln_b[i]),
            jnp.swapaxes(w_s[i], 1, 2).astype(MXU_DTYPE), b_s[i][:, None, :], tm)
        ys = _s5_mix(xb_m, wcat, ws, a_pow, i, segments)
        x_t = _mix_out(x_t, ya, xb, ys, sz, col(d_skip[i]), w_glu[i].T.astype(MXU_DTYPE),
                       col(b_glu[i]), w_out[i].T.astype(MXU_DTYPE), col(final_g), tm,
                       final_norm=(i == depth - 1))

    outs, col0 = [], 0
    for x in inputs:
        n, l, _ = x.shape
        outs.append(_from_chunk_major(x_t[:, col0:col0 + n * l], n, l))
        col0 += n * l
    return tuple(outs)
```

```python
import functools

import jax
import jax.numpy as jnp
from jax import lax
from jax.experimental import pallas as pl
from jax.experimental.pallas import tpu as pltpu

D_MODEL = 1024
WIDTH = 1024
CHUNK = 128
HEADS = 4
HEAD_DIM = WIDTH // HEADS
GROUP_CH = 16
GROUPS = WIDTH // GROUP_CH
STATE = 64
N_DIR = 2
EPS = 1e-6

SUBLANES = 8
ROW_TILES = WIDTH // SUBLANES
DIRSTATE = N_DIR * STATE
TOEP = GROUP_CH * CHUNK
VMEM_LIMIT_BYTES = 56 * 1024 * 1024

MXU_DTYPE = jnp.bfloat16
F32 = jnp.float32


def _const_spec(shape):
    zeros = (0,) * len(shape)
    return pl.BlockSpec(shape, lambda *_: zeros, pipeline_mode=pl.Buffered(1))


def _params(n_grid_axes):
    return pltpu.CompilerParams(
        dimension_semantics=("arbitrary",) * n_grid_axes,
        vmem_limit_bytes=VMEM_LIMIT_BYTES)


def _powers(lam_re, lam_im, dt, k):
    mag = jnp.exp(k * lam_re * dt)
    ang = k * lam_im * dt
    return mag * jnp.cos(ang), mag * jnp.sin(ang)


def _gen_kernel(lamr_c, lami_c, ldt_c, lamr_r, lami_r, ldt_r, crep_r, crep_i, brep_r, brep_i,
                bt_r, bt_i, ct_r, ct_i, wcat_ref, ws_ref, a_ref, lag_scr):
    lr, li, dt_r = lamr_r[0], lami_r[0], jnp.exp(ldt_r[0])
    lrc, lic, dt_c = lamr_c[0], lami_c[0], jnp.exp(ldt_c[0])

    abar_r, abar_i = _powers(lr, li, dt_r, 1.0)
    num_r, num_i = abar_r - 1.0, abar_i
    den = lr * lr + li * li
    f_r = (num_r * lr + num_i * li) / den
    f_i = (num_i * lr - num_r * li) / den

    a_r, a_i = _powers(lr, li, dt_r, float(CHUNK))
    a_ref[0, 0:1, :] = a_r
    a_ref[0, 1:2, :] = a_i

    bb_r = f_r * brep_r[0] - f_i * brep_i[0]
    bb_i = f_r * brep_i[0] + f_i * brep_r[0]
    cb_r = crep_r[0] * bb_r - crep_i[0] * bb_i
    cb_i = crep_r[0] * bb_i + crep_i[0] * bb_r
    lag_lhs = jnp.concatenate([cb_r, -cb_i], axis=1)
    m_lane = lax.broadcasted_iota(jnp.int32, (DIRSTATE, CHUNK), 1)
    row_bwd = lax.broadcasted_iota(jnp.int32, (DIRSTATE, CHUNK), 0) >= STATE
    tab_r, tab_i = _powers(lrc, lic, dt_c, jnp.where(row_bwd, CHUNK - m_lane, m_lane).astype(F32))
    lag0 = jnp.where(m_lane == 0, 1.0, 0.0)
    lag_rhs = jnp.concatenate(
        [jnp.concatenate([jnp.where(row_bwd, lag0, tab_r), jnp.where(row_bwd, tab_r, 0.0)], axis=1),
         jnp.concatenate([jnp.where(row_bwd, 0.0, tab_i), jnp.where(row_bwd, tab_i, 0.0)], axis=1)],
        axis=0)
    lag_scr[...] = jnp.dot(lag_lhs, lag_rhs, precision=lax.Precision.HIGHEST,
                           preferred_element_type=F32).reshape(lag_scr.shape)

    s_idx = lax.broadcasted_iota(jnp.int32, (CHUNK, CHUNK), 0)
    m_idx = lax.broadcasted_iota(jnp.int32, (CHUNK, CHUNK), 1)
    use_fwd = s_idx + m_idx <= CHUNK - 1

    def toeplitz_rows(c_in, carry):
        row0 = pl.multiple_of(c_in * CHUNK, CHUNK)
        for c_out in range(GROUP_CH):
            fwd = lag_scr[c_in, c_out:c_out + 1, 0:CHUNK]
            bwd = lag_scr[c_in, c_out:c_out + 1, CHUNK:2 * CHUNK]
            pre = jnp.where(use_fwd, jnp.broadcast_to(fwd, (CHUNK, CHUNK)),
                            jnp.broadcast_to(bwd, (CHUNK, CHUNK)))
            blk = pltpu.roll(pre, 0, 1, stride=1, stride_axis=0)
            wcat_ref[0, pl.ds(row0, CHUNK), c_out * CHUNK:(c_out + 1) * CHUNK] = (
                blk.astype(wcat_ref.dtype))
        return carry

    lax.fori_loop(0, GROUP_CH, toeplitz_rows, 0)

    s_pos = lax.broadcasted_iota(jnp.int32, (CHUNK, DIRSTATE), 0)
    lane_bwd = lax.broadcasted_iota(jnp.int32, (CHUNK, DIRSTATE), 1) >= STATE
    k_in = jnp.where(lane_bwd, s_pos, CHUNK - 1 - s_pos).astype(F32)
    pin_r, pin_i = _powers(lr, li, dt_r, k_in)
    bbt_r = f_r * bt_r[0] - f_i * bt_i[0]
    bbt_i = f_r * bt_i[0] + f_i * bt_r[0]
    for c_in in range(GROUP_CH):
        b_r, b_i = bbt_r[c_in:c_in + 1, :], bbt_i[c_in:c_in + 1, :]
        rows = slice(c_in * CHUNK, (c_in + 1) * CHUNK)
        ws_ref[0, rows, 0:DIRSTATE] = (pin_r * b_r - pin_i * b_i).astype(ws_ref.dtype)
        ws_ref[0, rows, DIRSTATE:2 * DIRSTATE] = (pin_r * b_i + pin_i * b_r).astype(ws_ref.dtype)

    abar_cr, abar_ci = _powers(lrc, lic, dt_c, 1.0)
    pout_r = jnp.where(row_bwd, tab_r, abar_cr * tab_r - abar_ci * tab_i)
    pout_i = jnp.where(row_bwd, tab_i, abar_cr * tab_i + abar_ci * tab_r)
    c_r_all, c_i_all = ct_r[0], ct_i[0]
    for c_out in range(GROUP_CH):
        c_r, c_i = c_r_all[:, c_out:c_out + 1], c_i_all[:, c_out:c_out + 1]
        cols = slice(c_out * CHUNK, (c_out + 1) * CHUNK)
        wcat_ref[0, TOEP:TOEP + DIRSTATE, cols] = (c_r * pout_r - c_i * pout_i).astype(wcat_ref.dtype)
        wcat_ref[0, TOEP + DIRSTATE:TOEP + 2 * DIRSTATE, cols] = (
            -(c_r * pout_i + c_i * pout_r)).astype(wcat_ref.dtype)


def _s5_operators(lam_re, lam_im, log_dt, b_re, b_im, c_re, c_im):
    depth = lam_re.shape[0]
    lg = depth * GROUPS

    def dirstate(a):
        return a.transpose(0, 2, 1, 3).reshape(lg, DIRSTATE)

    lamr, lami = dirstate(lam_re), dirstate(lam_im)
    ldt = dirstate(jnp.broadcast_to(log_dt[..., None], lam_re.shape))

    def b_layout(b):
        return b.transpose(0, 2, 4, 1, 3).reshape(depth, GROUPS, GROUP_CH, DIRSTATE)

    def c_layout(c):
        return c.transpose(0, 2, 3, 1, 4).reshape(depth, GROUPS, GROUP_CH, DIRSTATE)

    pair = (depth, GROUPS, GROUP_CH, GROUP_CH, DIRSTATE)

    def b_rep(b):
        return jnp.broadcast_to(b_layout(b)[:, :, :, None, :], pair).reshape(lg, GROUP_CH ** 2, DIRSTATE)

    def c_rep(c):
        return jnp.broadcast_to(c_layout(c)[:, :, None, :, :], pair).reshape(lg, GROUP_CH ** 2, DIRSTATE)

    def c_cols(c):
        return c.transpose(0, 2, 1, 4, 3).reshape(lg, DIRSTATE, GROUP_CH)

    args = (lamr[:, :, None], lami[:, :, None], ldt[:, :, None],
            lamr[:, None, :], lami[:, None, :], ldt[:, None, :],
            c_rep(c_re), c_rep(c_im), b_rep(b_re), b_rep(b_im),
            b_layout(b_re).reshape(lg, GROUP_CH, DIRSTATE), b_layout(b_im).reshape(lg, GROUP_CH, DIRSTATE),
            c_cols(c_re), c_cols(c_im))

    def spec(a):
        return pl.BlockSpec((1,) + a.shape[1:], lambda i: (i, 0, 0))

    return pl.pallas_call(
        _gen_kernel,
        grid=(lg,),
        in_specs=[spec(a) for a in args],
        out_specs=[pl.BlockSpec((1, TOEP + 2 * DIRSTATE, TOEP), lambda i: (i, 0, 0)),
                   pl.BlockSpec((1, TOEP, 2 * DIRSTATE), lambda i: (i, 0, 0)),
                   pl.BlockSpec((1, 2, DIRSTATE), lambda i: (i, 0, 0))],
        out_shape=[jax.ShapeDtypeStruct((lg, TOEP + 2 * DIRSTATE, TOEP), MXU_DTYPE),
                   jax.ShapeDtypeStruct((lg, TOEP, 2 * DIRSTATE), MXU_DTYPE),
                   jax.ShapeDtypeStruct((lg, 2, DIRSTATE), F32)],
        scratch_shapes=[pltpu.VMEM((GROUP_CH, GROUP_CH, 2 * CHUNK), F32)],
        compiler_params=_params(1),
        name="s5_operators",
    )(*args)


def _store_tile_major(ref, val, chunk0):
    for r in range(val.shape[0] // SUBLANES):
        for k in range(val.shape[1] // CHUNK):
            ref[r, (chunk0 + k) * SUBLANES:(chunk0 + k + 1) * SUBLANES, :] = (
                val[r * SUBLANES:(r + 1) * SUBLANES, k * CHUNK:(k + 1) * CHUNK])


def _load_tile_major(ref, chunk0, n_chunks):
    rows = [jnp.concatenate([ref[r, (chunk0 + k) * SUBLANES:(chunk0 + k + 1) * SUBLANES, :]
                             for k in range(n_chunks)], axis=1) for r in range(ref.shape[0])]
    return jnp.concatenate(rows, axis=0)


def _tile_major_spec(tm):
    return pl.BlockSpec((ROW_TILES, tm // CHUNK * SUBLANES, CHUNK), lambda i: (0, i, 0))


def _tile_major_shape(t):
    return jax.ShapeDtypeStruct((ROW_TILES, t // CHUNK * SUBLANES, CHUNK), F32)


def _mix_in_kernel(x_ref, ng_ref, win_ref, lng_ref, lnb_ref, wst_ref, bs_ref, ya_ref, xb_ref, sz_ref):
    x = x_ref[...]
    inv = lax.rsqrt(jnp.mean(x * x, axis=0, keepdims=True) + EPS)
    h = (x * inv * ng_ref[...]).astype(MXU_DTYPE)

    def proj(k):
        return jnp.dot(win_ref[k * WIDTH:(k + 1) * WIDTH, :], h, preferred_element_type=F32)

    p_v, p_xb, p_zb, p_u, p_za = proj(1), proj(3), proj(4), proj(0), proj(2)

    v = jax.nn.gelu(p_v)
    mu = jnp.mean(v, axis=0, keepdims=True)
    vc = v - mu
    var = jnp.mean(vc * vc, axis=0, keepdims=True)
    vn = (vc * lax.rsqrt(var + EPS) * lng_ref[...] + lnb_ref[...]).astype(MXU_DTYPE)

    _store_tile_major(xb_ref, p_xb, 0)
    sz_ref[...] = jax.nn.silu(p_zb)
    gated = jax.nn.gelu(p_u) * jax.nn.silu(p_za)

    n_chunks = x.shape[1] // CHUNK
    heads = []
    for pair in range(HEADS // 2):
        va = vn[(2 * pair) * HEAD_DIM:(2 * pair + 1) * HEAD_DIM, :]
        vb = vn[(2 * pair + 1) * HEAD_DIM:(2 * pair + 2) * HEAD_DIM, :]
        stacked = jnp.concatenate(
            [jnp.concatenate([va[:, k * CHUNK:(k + 1) * CHUNK], vb[:, k * CHUNK:(k + 1) * CHUNK]], axis=1)
             for k in range(n_chunks)], axis=0)
        mixed = jnp.dot(stacked, wst_ref[pair], preferred_element_type=F32) + bs_ref[pair]
        for half in range(2):
            heads.append(jnp.concatenate(
                [mixed[k * HEAD_DIM:(k + 1) * HEAD_DIM, half * CHUNK:(half + 1) * CHUNK]
                 for k in range(n_chunks)], axis=1))
    sv = jnp.concatenate(heads, axis=0)
    ya_ref[...] = (gated * sv).astype(ya_ref.dtype)


def _mix_in(x_t, ng, win_t, lng, lnb, wst, bs, tm):
    d, t = x_t.shape
    tile = lambda rows: pl.BlockSpec((rows, tm), lambda i: (0, i))
    return pl.pallas_call(
        _mix_in_kernel,
        grid=(t // tm,),
        in_specs=[tile(d), _const_spec(ng.shape), _const_spec(win_t.shape), _const_spec(lng.shape),
                  _const_spec(lnb.shape), _const_spec(wst.shape), _const_spec(bs.shape)],
        out_specs=[tile(WIDTH), _tile_major_spec(tm), tile(WIDTH)],
        out_shape=[jax.ShapeDtypeStruct((WIDTH, t), MXU_DTYPE),
                   _tile_major_shape(t),
                   jax.ShapeDtypeStruct((WIDTH, t), F32)],
        compiler_params=_params(1),
        name="mix_in",
    )(x_t, ng, win_t, lng, lnb, wst, bs)


def _chunk_scan(s_re, s_im, a_r, a_i, n_chunks, n_seq, is_fwd):
    h_r = jnp.zeros((n_seq, DIRSTATE), F32)
    h_i = jnp.zeros((n_seq, DIRSTATE), F32)
    fwd_r, fwd_i = [h_r] * n_chunks, [h_i] * n_chunks
    bwd_r, bwd_i = [h_r] * n_chunks, [h_i] * n_chunks
    rows = lambda a, j: a[j * n_seq:(j + 1) * n_seq, :]
    for i in range(n_chunks - 1):
        jf, jb = i, n_chunks - 1 - i
        in_r = jnp.where(is_fwd, rows(s_re, jf), rows(s_re, jb))
        in_i = jnp.where(is_fwd, rows(s_im, jf), rows(s_im, jb))
        h_r, h_i = a_r * h_r - a_i * h_i + in_r, a_r * h_i + a_i * h_r + in_i
        fwd_r[jf + 1], fwd_i[jf + 1] = h_r, h_i
        bwd_r[jb - 1], bwd_i[jb - 1] = h_r, h_i
    out_r = jnp.concatenate([jnp.where(is_fwd, f, b) for f, b in zip(fwd_r, bwd_r)], axis=0)
    out_i = jnp.concatenate([jnp.where(is_fwd, f, b) for f, b in zip(fwd_i, bwd_i)], axis=0)
    return out_r, out_i


def _s5_kernel(x_ref, wcat_ref, ws_ref, a_ref, o_ref, *, segments):
    n_rows = x_ref.shape[1] // SUBLANES

    def channel(c):
        return c // SUBLANES, pl.ds(c % SUBLANES, n_rows, stride=SUBLANES)

    def load(c):
        tile, rows = channel(c)
        return x_ref[tile, rows, :].astype(MXU_DTYPE)

    lhs = jnp.concatenate([load(c) for c in range(GROUP_CH)], axis=1)
    s = jnp.dot(lhs, ws_ref[0], preferred_element_type=F32)
    a_r, a_i = a_ref[0, 0:1, :], a_ref[0, 1:2, :]
    is_fwd = lax.broadcasted_iota(jnp.int32, (1, DIRSTATE), 1) < STATE
    parts_r, parts_i = [], []
    for row0, n_chunks, n_seq in segments:
        seg = s[row0:row0 + n_chunks * n_seq, :]
        h_r, h_i = _chunk_scan(seg[:, :DIRSTATE], seg[:, DIRSTATE:], a_r, a_i, n_chunks, n_seq, is_fwd)
        parts_r.append(h_r)
        parts_i.append(h_i)
    h = jnp.concatenate([jnp.concatenate(parts_r, axis=0), jnp.concatenate(parts_i, axis=0)], axis=1)
    lhs2 = jnp.concatenate([lhs, h.astype(lhs.dtype)], axis=1)
    for pair in range(GROUP_CH // 2):
        y = jnp.dot(lhs2, wcat_ref[0, :, pair * 2 * CHUNK:(pair + 1) * 2 * CHUNK],
                    preferred_element_type=F32)
        for half in range(2):
            tile, rows = channel(2 * pair + half)
            o_ref[tile, rows, :] = y[:, half * CHUNK:(half + 1) * CHUNK]


def _s5_mix(xb, wcat, ws, a_pow, layer, segments):
    group_tiles = GROUP_CH // SUBLANES
    block = pl.BlockSpec((group_tiles,) + xb.shape[1:], lambda i: (i, 0, 0))
    group = lambda i: (layer * GROUPS + i, 0, 0)
    return pl.pallas_call(
        functools.partial(_s5_kernel, segments=segments),
        grid=(GROUPS,),
        in_specs=[block,
                  pl.BlockSpec((1,) + wcat.shape[1:], group),
                  pl.BlockSpec((1,) + ws.shape[1:], group),
                  pl.BlockSpec((1,) + a_pow.shape[1:], group)],
        out_specs=block,
        out_shape=jax.ShapeDtypeStruct(xb.shape, F32),
        compiler_params=_params(1),
        name="s5_mix",
    )(xb, wcat, ws, a_pow)


def _mix_out_kernel(x_ref, ya_ref, xb_ref, ys_ref, sz_ref, dsk_ref, wg_ref, bg_ref, wo_ref, fg_ref,
                    o_ref, *, final_norm):
    n_chunks = x_ref.shape[1] // CHUNK
    out = x_ref[...] + jnp.dot(wo_ref[:, :WIDTH], ya_ref[...], preferred_element_type=F32)
    y = _load_tile_major(ys_ref, 0, n_chunks) + dsk_ref[...] * _load_tile_major(xb_ref, 0, n_chunks)
    gy = jax.nn.gelu(y)
    gate = jax.nn.sigmoid(
        jnp.dot(wg_ref[...], gy.astype(MXU_DTYPE), preferred_element_type=F32) + bg_ref[...])
    yb = (gy * gate * sz_ref[...]).astype(MXU_DTYPE)
    out = out + jnp.dot(wo_ref[:, WIDTH:], yb, preferred_element_type=F32)
    if final_norm:
        out = out * lax.rsqrt(jnp.mean(out * out, axis=0, keepdims=True) + EPS) * fg_ref[...]
    o_ref[...] = out


def _mix_out(x_t, ya, xb, ys, sz, dsk, wg_t, bg, wo_t, fg, tm, final_norm):
    d, t = x_t.shape
    tile = lambda rows: pl.BlockSpec((rows, tm), lambda i: (0, i))
    return pl.pallas_call(
        functools.partial(_mix_out_kernel, final_norm=final_norm),
        grid=(t // tm,),
        in_specs=[tile(d), tile(WIDTH), _tile_major_spec(tm), _tile_major_spec(tm), tile(WIDTH),
                  _const_spec(dsk.shape), _const_spec(wg_t.shape), _const_spec(bg.shape),
                  _const_spec(wo_t.shape), _const_spec(fg.shape)],
        out_specs=tile(d),
        out_shape=jax.ShapeDtypeStruct((d, t), F32),
        compiler_params=_params(1),
        name="mix_out",
    )(x_t, ya, xb, ys, sz, dsk, wg_t, bg, wo_t, fg)


def _to_chunk_major(x):
    n, l, d = x.shape
    return x.reshape(n, l // CHUNK, CHUNK, d).transpose(3, 1, 0, 2).reshape(d, l * n)


def _from_chunk_major(x_t, n, l):
    d = x_t.shape[0]
    return x_t.reshape(d, l // CHUNK, n, CHUNK).transpose(2, 1, 3, 0).reshape(n, l, d)


def _token_tile(t):
    for tm in (512, 256, 128):
        if t % tm == 0:
            return tm
    raise ValueError(f"token count {t} is not a multiple of {CHUNK}")


def _pair_block_diag(w_s_layer):
    wt = jnp.swapaxes(w_s_layer, 1, 2)
    zero = jnp.zeros_like(wt[0])
    return jnp.stack([jnp.block([[wt[2 * k], zero], [zero, wt[2 * k + 1]]])
                      for k in range(HEADS // 2)])


def kernel(x_prompt, x_sample, norm_g, w_in, ln_g, ln_b, w_s, b_s, lam_re, lam_im, log_dt,
           b_re, b_im, c_re, c_im, d_skip, w_glu, b_glu, w_out, final_g):
    depth = norm_g.shape[0]
    inputs = (x_prompt, x_sample)
    for x in inputs:
        assert x.shape[1] % CHUNK == 0 and x.shape[2] == D_MODEL
    x_t = jnp.concatenate([_to_chunk_major(x) for x in inputs], axis=1)
    t = x_t.shape[1]
    tm = _token_tile(t)

    segments, row0 = [], 0
    for x in inputs:
        n_seq, n_chunks = x.shape[0], x.shape[1] // CHUNK
        segments.append((row0, n_chunks, n_seq))
        row0 += n_chunks * n_seq
    segments = tuple(segments)

    wcat, ws, a_pow = _s5_operators(lam_re, lam_im, log_dt, b_re, b_im, c_re, c_im)

    col = lambda a: a[:, None]
    for i in range(depth):
        ya, xb, sz = _mix_in(
            x_t, col(norm_g[i]), w_in[i].T.astype(MXU_DTYPE), col(ln_g[i]), col(ln_b[i]),
            _pair_block_diag(w_s[i]).astype(MXU_DTYPE), b_s[i].reshape(HEADS // 2, 1, 2 * CHUNK), tm)
        ys = _s5_mix(xb, wcat, ws, a_pow, i, segments)
        x_t = _mix_out(x_t, ya, xb, ys, sz, col(d_skip[i]), w_glu[i].T.astype(MXU_DTYPE),
                       col(b_glu[i]), w_out[i].T.astype(MXU_DTYPE), col(final_g), tm,
                       final_norm=(i == depth - 1))

    outs, col0 = [], 0
    for x in inputs:
        n, l, _ = x.shape
        outs.append(_from_chunk_major(x_t[:, col0:col0 + n * l], n, l))
        col0 += n * l
    return tuple(outs)
```

```python
import functools

import jax
import jax.numpy as jnp
from jax import lax
from jax.experimental import pallas as pl
from jax.experimental.pallas import tpu as pltpu

D_MODEL = 1024
WIDTH = 1024
CHUNK = 128
HEADS = 4
HEAD_DIM = WIDTH // HEADS
GROUP_CH = 16
GROUPS = WIDTH // GROUP_CH
STATE = 64
N_DIR = 2
EPS = 1e-6

SUBLANES = 8
ROW_TILES = WIDTH // SUBLANES
DIRSTATE = N_DIR * STATE
TOEP = GROUP_CH * CHUNK
VMEM_LIMIT_BYTES = 56 * 1024 * 1024

MXU_DTYPE = jnp.bfloat16
F32 = jnp.float32


def _const_spec(shape):
    zeros = (0,) * len(shape)
    return pl.BlockSpec(shape, lambda *_: zeros, pipeline_mode=pl.Buffered(1))


def _params(n_grid_axes):
    return pltpu.CompilerParams(
        dimension_semantics=("arbitrary",) * n_grid_axes,
        vmem_limit_bytes=VMEM_LIMIT_BYTES)


def _powers(lam_re, lam_im, dt, k):
    mag = jnp.exp(k * lam_re * dt)
    ang = k * lam_im * dt
    return mag * jnp.cos(ang), mag * jnp.sin(ang)


def _gen_kernel(lamr_c, lami_c, ldt_c, lamr_r, lami_r, ldt_r, crep_r, crep_i, brep_r, brep_i,
                bt_r, bt_i, ct_r, ct_i, wcat_ref, ws_ref, a_ref, lag_scr):
    lr, li, dt_r = lamr_r[0], lami_r[0], jnp.exp(ldt_r[0])
    lrc, lic, dt_c = lamr_c[0], lami_c[0], jnp.exp(ldt_c[0])

    abar_r, abar_i = _powers(lr, li, dt_r, 1.0)
    num_r, num_i = abar_r - 1.0, abar_i
    den = lr * lr + li * li
    f_r = (num_r * lr + num_i * li) / den
    f_i = (num_i * lr - num_r * li) / den

    a_r, a_i = _powers(lr, li, dt_r, float(CHUNK))
    a_ref[0, 0:1, :] = a_r
    a_ref[0, 1:2, :] = a_i

    bb_r = f_r * brep_r[0] - f_i * brep_i[0]
    bb_i = f_r * brep_i[0] + f_i * brep_r[0]
    cb_r = crep_r[0] * bb_r - crep_i[0] * bb_i
    cb_i = crep_r[0] * bb_i + crep_i[0] * bb_r
    lag_lhs = jnp.concatenate([cb_r, -cb_i], axis=1)
    m_lane = lax.broadcasted_iota(jnp.int32, (DIRSTATE, CHUNK), 1)
    row_bwd = lax.broadcasted_iota(jnp.int32, (DIRSTATE, CHUNK), 0) >= STATE
    tab_r, tab_i = _powers(lrc, lic, dt_c, jnp.where(row_bwd, CHUNK - m_lane, m_lane).astype(F32))
    lag0 = jnp.where(m_lane == 0, 1.0, 0.0)
    lag_rhs = jnp.concatenate(
        [jnp.concatenate([jnp.where(row_bwd, lag0, tab_r), jnp.where(row_bwd, tab_r, 0.0)], axis=1),
         jnp.concatenate([jnp.where(row_bwd, 0.0, tab_i), jnp.where(row_bwd, tab_i, 0.0)], axis=1)],
        axis=0)
    lag_scr[...] = jnp.dot(lag_lhs, lag_rhs, precision=lax.Precision.HIGHEST,
                           preferred_element_type=F32).reshape(lag_scr.shape)

    s_idx = lax.broadcasted_iota(jnp.int32, (CHUNK, CHUNK), 0)
    m_idx = lax.broadcasted_iota(jnp.int32, (CHUNK, CHUNK), 1)
    use_fwd = s_idx + m_idx <= CHUNK - 1

    def toeplitz_rows(c_in, carry):
        row0 = pl.multiple_of(c_in * CHUNK, CHUNK)
        for c_out in range(GROUP_CH):
            fwd = lag_scr[c_in, c_out:c_out + 1, 0:CHUNK]
            bwd = lag_scr[c_in, c_out:c_out + 1, CHUNK:2 * CHUNK]
            pre = jnp.where(use_fwd, jnp.broadcast_to(fwd, (CHUNK, CHUNK)),
                            jnp.broadcast_to(bwd, (CHUNK, CHUNK)))
            blk = pltpu.roll(pre, 0, 1, stride=1, stride_axis=0)
            wcat_ref[0, pl.ds(row0, CHUNK), c_out * CHUNK:(c_out + 1) * CHUNK] = (
                blk.astype(wcat_ref.dtype))
        return carry

    lax.fori_loop(0, GROUP_CH, toeplitz_rows, 0)

    s_pos = lax.broadcasted_iota(jnp.int32, (CHUNK, DIRSTATE), 0)
    lane_bwd = lax.broadcasted_iota(jnp.int32, (CHUNK, DIRSTATE), 1) >= STATE
    k_in = jnp.where(lane_bwd, s_pos, CHUNK - 1 - s_pos).astype(F32)
    pin_r, pin_i = _powers(lr, li, dt_r, k_in)
    bbt_r = f_r * bt_r[0] - f_i * bt_i[0]
    bbt_i = f_r * bt_i[0] + f_i * bt_r[0]
    for c_in in range(GROUP_CH):
        b_r, b_i = bbt_r[c_in:c_in + 1, :], bbt_i[c_in:c_in + 1, :]
        rows = slice(c_in * CHUNK, (c_in + 1) * CHUNK)
        ws_ref[0, rows, 0:DIRSTATE] = (pin_r * b_r - pin_i * b_i).astype(ws_ref.dtype)
        ws_ref[0, rows, DIRSTATE:2 * DIRSTATE] = (pin_r * b_i + pin_i * b_r).astype(ws_ref.dtype)

    abar_cr, abar_ci = _powers(lrc, lic, dt_c, 1.0)
    pout_r = jnp.where(row_bwd, tab_r, abar_cr * tab_r - abar_ci * tab_i)
    pout_i = jnp.where(row_bwd, tab_i, abar_cr * tab_i + abar_ci * tab_r)
    c_r_all, c_i_all = ct_r[0], ct_i[0]
    for c_out in range(GROUP_CH):
        c_r, c_i = c_r_all[:, c_out:c_out + 1], c_i_all[:, c_out:c_out + 1]
        cols = slice(c_out * CHUNK, (c_out + 1) * CHUNK)
        wcat_ref[0, TOEP:TOEP + DIRSTATE, cols] = (c_r * pout_r - c_i * pout_i).astype(wcat_ref.dtype)
        wcat_ref[0, TOEP + DIRSTATE:TOEP + 2 * DIRSTATE, cols] = (
            -(c_r * pout_i + c_i * pout_r)).astype(wcat_ref.dtype)


def _s5_operators(lam_re, lam_im, log_dt, b_re, b_im, c_re, c_im):
    depth = lam_re.shape[0]
    lg = depth * GROUPS

    def dirstate(a):
        return a.transpose(0, 2, 1, 3).reshape(lg, DIRSTATE)

    lamr, lami = dirstate(lam_re), dirstate(lam_im)
    ldt = dirstate(jnp.broadcast_to(log_dt[..., None], lam_re.shape))

    def b_layout(b):
        return b.transpose(0, 2, 4, 1, 3).reshape(depth, GROUPS, GROUP_CH, DIRSTATE)

    def c_layout(c):
        return c.transpose(0, 2, 3, 1, 4).reshape(depth, GROUPS, GROUP_CH, DIRSTATE)

    pair = (depth, GROUPS, GROUP_CH, GROUP_CH, DIRSTATE)

    def b_rep(b):
        return jnp.broadcast_to(b_layout(b)[:, :, :, None, :], pair).reshape(lg, GROUP_CH ** 2, DIRSTATE)

    def c_rep(c):
        return jnp.broadcast_to(c_layout(c)[:, :, None, :, :], pair).reshape(lg, GROUP_CH ** 2, DIRSTATE)

    def c_cols(c):
        return c.transpose(0, 2, 1, 4, 3).reshape(lg, DIRSTATE, GROUP_CH)

    args = (lamr[:, :, None], lami[:, :, None], ldt[:, :, None],
            lamr[:, None, :], lami[:, None, :], ldt[:, None, :],
            c_rep(c_re), c_rep(c_im), b_rep(b_re), b_rep(b_im),
            b_layout(b_re).reshape(lg, GROUP_CH, DIRSTATE), b_layout(b_im).reshape(lg, GROUP_CH, DIRSTATE),
            c_cols(c_re), c_cols(c_im))

    def spec(a):
        return pl.BlockSpec((1,) + a.shape[1:], lambda i: (i, 0, 0))

    return pl.pallas_call(
        _gen_kernel,
        grid=(lg,),
        in_specs=[spec(a) for a in args],
        out_specs=[pl.BlockSpec((1, TOEP + 2 * DIRSTATE, TOEP), lambda i: (i, 0, 0)),
                   pl.BlockSpec((1, TOEP, 2 * DIRSTATE), lambda i: (i, 0, 0)),
                   pl.BlockSpec((1, 2, DIRSTATE), lambda i: (i, 0, 0))],
        out_shape=[jax.ShapeDtypeStruct((lg, TOEP + 2 * DIRSTATE, TOEP), MXU_DTYPE),
                   jax.ShapeDtypeStruct((lg, TOEP, 2 * DIRSTATE), MXU_DTYPE),
                   jax.ShapeDtypeStruct((lg, 2, DIRSTATE), F32)],
        scratch_shapes=[pltpu.VMEM((GROUP_CH, GROUP_CH, 2 * CHUNK), F32)],
        compiler_params=_params(1),
        name="s5_operators",
    )(*args)


def _store_tile_major(ref, val, chunk0):
    for r in range(val.shape[0] // SUBLANES):
        for k in range(val.shape[1] // CHUNK):
            ref[r, (chunk0 + k) * SUBLANES:(chunk0 + k + 1) * SUBLANES, :] = (
                val[r * SUBLANES:(r + 1) * SUBLANES, k * CHUNK:(k + 1) * CHUNK])


def _load_tile_major(ref, chunk0, n_chunks):
    rows = [jnp.concatenate([ref[r, (chunk0 + k) * SUBLANES:(chunk0 + k + 1) * SUBLANES, :]
                             for k in range(n_chunks)], axis=1) for r in range(ref.shape[0])]
    return jnp.concatenate(rows, axis=0)


def _tile_major_spec(tm):
    return pl.BlockSpec((ROW_TILES, tm // CHUNK * SUBLANES, CHUNK), lambda i: (0, i, 0))


def _tile_major_shape(t):
    return jax.ShapeDtypeStruct((ROW_TILES, t // CHUNK * SUBLANES, CHUNK), F32)


def _mix_in_body(x, ng_ref, win_ref, lng_ref, lnb_ref, wst_ref, bs_ref, ya_ref, xb_ref, sz_ref):
    inv = lax.rsqrt(jnp.mean(x * x, axis=0, keepdims=True) + EPS)
    h = (x * inv * ng_ref[...]).astype(MXU_DTYPE)

    def proj(k):
        return jnp.dot(win_ref[k * WIDTH:(k + 1) * WIDTH, :], h, preferred_element_type=F32)

    p_v, p_xb, p_zb, p_u, p_za = proj(1), proj(3), proj(4), proj(0), proj(2)

    v = jax.nn.gelu(p_v)
    mu = jnp.mean(v, axis=0, keepdims=True)
    vc = v - mu
    var = jnp.mean(vc * vc, axis=0, keepdims=True)
    vn = (vc * lax.rsqrt(var + EPS) * lng_ref[...] + lnb_ref[...]).astype(MXU_DTYPE)

    _store_tile_major(xb_ref, p_xb, 0)
    sz_ref[...] = jax.nn.silu(p_zb)
    gated = jax.nn.gelu(p_u) * jax.nn.silu(p_za)

    n_chunks = x.shape[1] // CHUNK
    heads = []
    for pair in range(HEADS // 2):
        va = vn[(2 * pair) * HEAD_DIM:(2 * pair + 1) * HEAD_DIM, :]
        vb = vn[(2 * pair + 1) * HEAD_DIM:(2 * pair + 2) * HEAD_DIM, :]
        stacked = jnp.concatenate(
            [jnp.concatenate([va[:, k * CHUNK:(k + 1) * CHUNK], vb[:, k * CHUNK:(k + 1) * CHUNK]], axis=1)
             for k in range(n_chunks)], axis=0)
        mixed = jnp.dot(stacked, wst_ref[pair], preferred_element_type=F32) + bs_ref[pair]
        for half in range(2):
            heads.append(jnp.concatenate(
                [mixed[k * HEAD_DIM:(k + 1) * HEAD_DIM, half * CHUNK:(half + 1) * CHUNK]
                 for k in range(n_chunks)], axis=1))
    sv = jnp.concatenate(heads, axis=0)
    ya_ref[...] = (gated * sv).astype(ya_ref.dtype)


def _mix_in_kernel(x_ref, *refs):
    _mix_in_body(x_ref[...], *refs)


def _mix_in_natural_kernel(xin_ref, *refs, n_carried):
    weights, outs = refs[:6], refs[6 + n_carried:]
    ya_ref, xb_ref, sz_ref, xt_ref = outs
    x = jnp.concatenate([xin_ref[u].T for u in range(xin_ref.shape[0])], axis=1)
    xt_ref[...] = x
    _mix_in_body(x, *weights, ya_ref, xb_ref, sz_ref)


def _mix_in_out_shapes(t):
    return [jax.ShapeDtypeStruct((WIDTH, t), MXU_DTYPE),
            _tile_major_shape(t),
            jax.ShapeDtypeStruct((WIDTH, t), F32)]


def _mix_in(x_t, weights, tm):
    d, t = x_t.shape
    tile = lambda rows: pl.BlockSpec((rows, tm), lambda i: (0, i))
    return pl.pallas_call(
        _mix_in_kernel,
        grid=(t // tm,),
        in_specs=[tile(d)] + [_const_spec(w.shape) for w in weights],
        out_specs=[tile(WIDTH), _tile_major_spec(tm), tile(WIDTH)],
        out_shape=_mix_in_out_shapes(t),
        compiler_params=_params(1),
        name="mix_in",
    )(x_t, *weights)


def _natural_spec(n_seq, units, d):
    per_chunk = n_seq // units
    return pl.BlockSpec((units, CHUNK, d), lambda i: (i % per_chunk, i // per_chunk, 0))


def _mix_in_natural(x_nat, weights, tile0, t, units, carried):
    n_seq, l, d = x_nat.shape
    tm = units * CHUNK
    tile = lambda rows: pl.BlockSpec((rows, tm), lambda i: (0, tile0 + i))
    tile_major = pl.BlockSpec((ROW_TILES, units * SUBLANES, CHUNK), lambda i: (0, tile0 + i, 0))
    first_carried = 1 + len(weights)
    return pl.pallas_call(
        functools.partial(_mix_in_natural_kernel, n_carried=len(carried)),
        grid=(n_seq * l // tm,),
        in_specs=([_natural_spec(n_seq, units, d)] + [_const_spec(w.shape) for w in weights]
                  + [pl.BlockSpec(memory_space=pl.ANY)] * len(carried)),
        out_specs=[tile(WIDTH), tile_major, tile(WIDTH), tile(d)],
        out_shape=_mix_in_out_shapes(t) + [jax.ShapeDtypeStruct((d, t), F32)],
        input_output_aliases={first_carried + k: k for k in range(len(carried))},
        compiler_params=_params(1),
        name="mix_in_natural",
    )(x_nat, *weights, *carried)


def _chunk_scan(s_re, s_im, a_r, a_i, n_chunks, n_seq, is_fwd):
    h_r = jnp.zeros((n_seq, DIRSTATE), F32)
    h_i = jnp.zeros((n_seq, DIRSTATE), F32)
    fwd_r, fwd_i = [h_r] * n_chunks, [h_i] * n_chunks
    bwd_r, bwd_i = [h_r] * n_chunks, [h_i] * n_chunks
    rows = lambda a, j: a[j * n_seq:(j + 1) * n_seq, :]
    for i in range(n_chunks - 1):
        jf, jb = i, n_chunks - 1 - i
        in_r = jnp.where(is_fwd, rows(s_re, jf), rows(s_re, jb))
        in_i = jnp.where(is_fwd, rows(s_im, jf), rows(s_im, jb))
        h_r, h_i = a_r * h_r - a_i * h_i + in_r, a_r * h_i + a_i * h_r + in_i
        fwd_r[jf + 1], fwd_i[jf + 1] = h_r, h_i
        bwd_r[jb - 1], bwd_i[jb - 1] = h_r, h_i
    out_r = jnp.concatenate([jnp.where(is_fwd, f, b) for f, b in zip(fwd_r, bwd_r)], axis=0)
    out_i = jnp.concatenate([jnp.where(is_fwd, f, b) for f, b in zip(fwd_i, bwd_i)], axis=0)
    return out_r, out_i


def _s5_kernel(x_ref, wcat_ref, ws_ref, a_ref, o_ref, *, segments):
    n_rows = x_ref.shape[1] // SUBLANES

    def channel(c):
        return c // SUBLANES, pl.ds(c % SUBLANES, n_rows, stride=SUBLANES)

    def load(c):
        tile, rows = channel(c)
        return x_ref[tile, rows, :].astype(MXU_DTYPE)

    lhs = jnp.concatenate([load(c) for c in range(GROUP_CH)], axis=1)
    s = jnp.dot(lhs, ws_ref[0], preferred_element_type=F32)
    a_r, a_i = a_ref[0, 0:1, :], a_ref[0, 1:2, :]
    is_fwd = lax.broadcasted_iota(jnp.int32, (1, DIRSTATE), 1) < STATE
    parts_r, parts_i = [], []
    for row0, n_chunks, n_seq in segments:
        seg = s[row0:row0 + n_chunks * n_seq, :]
        h_r, h_i = _chunk_scan(seg[:, :DIRSTATE], seg[:, DIRSTATE:], a_r, a_i, n_chunks, n_seq, is_fwd)
        parts_r.append(h_r)
        parts_i.append(h_i)
    h = jnp.concatenate([jnp.concatenate(parts_r, axis=0), jnp.concatenate(parts_i, axis=0)], axis=1)
    lhs2 = jnp.concatenate([lhs, h.astype(lhs.dtype)], axis=1)
    for pair in range(GROUP_CH // 2):
        y = jnp.dot(lhs2, wcat_ref[0, :, pair * 2 * CHUNK:(pair + 1) * 2 * CHUNK],
                    preferred_element_type=F32)
        for half in range(2):
            tile, rows = channel(2 * pair + half)
            o_ref[tile, rows, :] = y[:, half * CHUNK:(half + 1) * CHUNK]


def _s5_mix(xb, wcat, ws, a_pow, layer, segments):
    group_tiles = GROUP_CH // SUBLANES
    block = pl.BlockSpec((group_tiles,) + xb.shape[1:], lambda i: (i, 0, 0))
    group = lambda i: (layer * GROUPS + i, 0, 0)
    return pl.pallas_call(
        functools.partial(_s5_kernel, segments=segments),
        grid=(GROUPS,),
        in_specs=[block,
                  pl.BlockSpec((1,) + wcat.shape[1:], group),
                  pl.BlockSpec((1,) + ws.shape[1:], group),
                  pl.BlockSpec((1,) + a_pow.shape[1:], group)],
        out_specs=block,
        out_shape=jax.ShapeDtypeStruct(xb.shape, F32),
        compiler_params=_params(1),
        name="s5_mix",
    )(xb, wcat, ws, a_pow)


def _mix_out_body(x_ref, ya_ref, xb_ref, ys_ref, sz_ref, dsk_ref, wg_ref, bg_ref, wo_ref):
    n_chunks = x_ref.shape[1] // CHUNK
    out = x_ref[...] + jnp.dot(wo_ref[:, :WIDTH], ya_ref[...], preferred_element_type=F32)
    y = _load_tile_major(ys_ref, 0, n_chunks) + dsk_ref[...] * _load_tile_major(xb_ref, 0, n_chunks)
    gy = jax.nn.gelu(y)
    gate = jax.nn.sigmoid(
        jnp.dot(wg_ref[...], gy.astype(MXU_DTYPE), preferred_element_type=F32) + bg_ref[...])
    yb = (gy * gate * sz_ref[...]).astype(MXU_DTYPE)
    return out + jnp.dot(wo_ref[:, WIDTH:], yb, preferred_element_type=F32)


def _mix_out_kernel(*refs):
    refs[-1][...] = _mix_out_body(*refs[:-1])


def _mix_out_final_kernel(*refs):
    fg_ref, o_ref = refs[-2:]
    out = _mix_out_body(*refs[:-2])
    out = out * lax.rsqrt(jnp.mean(out * out, axis=0, keepdims=True) + EPS) * fg_ref[...]
    for u in range(o_ref.shape[0]):
        o_ref[u] = out[:, u * CHUNK:(u + 1) * CHUNK].T


def _mix_out(x_t, ya, xb, ys, sz, weights, tm):
    d, t = x_t.shape
    tile = lambda rows: pl.BlockSpec((rows, tm), lambda i: (0, i))
    return pl.pallas_call(
        _mix_out_kernel,
        grid=(t // tm,),
        in_specs=([tile(d), tile(WIDTH), _tile_major_spec(tm), _tile_major_spec(tm), tile(WIDTH)]
                  + [_const_spec(w.shape) for w in weights]),
        out_specs=tile(d),
        out_shape=jax.ShapeDtypeStruct((d, t), F32),
        compiler_params=_params(1),
        name="mix_out",
    )(x_t, ya, xb, ys, sz, *weights)


def _mix_out_final(x_t, ya, xb, ys, sz, weights, tile0, units, n_seq, l):
    d = x_t.shape[0]
    tm = units * CHUNK
    tile = lambda rows: pl.BlockSpec((rows, tm), lambda i: (0, tile0 + i))
    tile_major = pl.BlockSpec((ROW_TILES, units * SUBLANES, CHUNK), lambda i: (0, tile0 + i, 0))
    return pl.pallas_call(
        _mix_out_final_kernel,
        grid=(n_seq * l // tm,),
        in_specs=([tile(d), tile(WIDTH), tile_major, tile_major, tile(WIDTH)]
                  + [_const_spec(w.shape) for w in weights]),
        out_specs=_natural_spec(n_seq, units, d),
        out_shape=jax.ShapeDtypeStruct((n_seq, l, d), F32),
        compiler_params=_params(1),
        name="mix_out_final",
    )(x_t, ya, xb, ys, sz, *weights)


def _units_per_tile(inputs):
    for units in (4, 2, 1):
        if all(x.shape[0] % units == 0 for x in inputs):
            return units


def _pair_block_diag(w_s_layer):
    wt = jnp.swapaxes(w_s_layer, 1, 2)
    zero = jnp.zeros_like(wt[0])
    return jnp.stack([jnp.block([[wt[2 * k], zero], [zero, wt[2 * k + 1]]])
                      for k in range(HEADS // 2)])


def kernel(x_prompt, x_sample, norm_g, w_in, ln_g, ln_b, w_s, b_s, lam_re, lam_im, log_dt,
           b_re, b_im, c_re, c_im, d_skip, w_glu, b_glu, w_out, final_g):
    depth = norm_g.shape[0]
    inputs = (x_prompt, x_sample)
    for x in inputs:
        assert x.shape[1] % CHUNK == 0 and x.shape[2] == D_MODEL
    units = _units_per_tile(inputs)
    tm = units * CHUNK
    t = sum(x.shape[0] * x.shape[1] for x in inputs)

    segments, tile0s, row0 = [], [], 0
    for x in inputs:
        n_seq, n_chunks = x.shape[0], x.shape[1] // CHUNK
        segments.append((row0, n_chunks, n_seq))
        tile0s.append(row0 // units)
        row0 += n_chunks * n_seq
    segments = tuple(segments)

    wcat, ws, a_pow = _s5_operators(lam_re, lam_im, log_dt, b_re, b_im, c_re, c_im)

    col = lambda a: a[:, None]
    x_t = None
    for i in range(depth):
        in_weights = (col(norm_g[i]), w_in[i].T.astype(MXU_DTYPE), col(ln_g[i]), col(ln_b[i]),
                      _pair_block_diag(w_s[i]).astype(MXU_DTYPE),
                      b_s[i].reshape(HEADS // 2, 1, 2 * CHUNK))
        out_weights = (col(d_skip[i]), w_glu[i].T.astype(MXU_DTYPE), col(b_glu[i]),
                       w_out[i].T.astype(MXU_DTYPE))
        if i == 0:
            carried = ()
            for x, tile0 in zip(inputs, tile0s):
                carried = _mix_in_natural(x, in_weights, tile0, t, units, carried)
            ya, xb, sz, x_t = carried
        else:
            ya, xb, sz = _mix_in(x_t, in_weights, tm)
        ys = _s5_mix(xb, wcat, ws, a_pow, i, segments)
        if i < depth - 1:
            x_t = _mix_out(x_t, ya, xb, ys, sz, out_weights, tm)
    return tuple(
        _mix_out_final(x_t, ya, xb, ys, sz, out_weights + (col(final_g),), tile0, units,
                       x.shape[0], x.shape[1])
        for x, tile0 in zip(inputs, tile0s))
```

```python
import functools

import jax
import jax.numpy as jnp
from jax import lax
from jax.experimental import pallas as pl
from jax.experimental.pallas import tpu as pltpu

D_MODEL = 1024
WIDTH = 1024
CHUNK = 128
HEADS = 4
HEAD_DIM = WIDTH // HEADS
GROUP_CH = 16
GROUPS = WIDTH // GROUP_CH
STATE = 64
N_DIR = 2
EPS = 1e-6

SUBLANES = 8
ROW_TILES = WIDTH // SUBLANES
DIRSTATE = N_DIR * STATE
TOEP = GROUP_CH * CHUNK
VMEM_LIMIT_BYTES = 56 * 1024 * 1024

MXU_DTYPE = jnp.bfloat16
F32 = jnp.float32


def _const_spec(shape):
    zeros = (0,) * len(shape)
    return pl.BlockSpec(shape, lambda *_: zeros, pipeline_mode=pl.Buffered(1))


def _params(n_grid_axes):
    return pltpu.CompilerParams(
        dimension_semantics=("arbitrary",) * n_grid_axes,
        vmem_limit_bytes=VMEM_LIMIT_BYTES)


def _powers(lam_re, lam_im, dt, k):
    mag = jnp.exp(k * lam_re * dt)
    ang = k * lam_im * dt
    return mag * jnp.cos(ang), mag * jnp.sin(ang)


def _gen_kernel(lamr_c, lami_c, ldt_c, lamr_r, lami_r, ldt_r, crep_r, crep_i, brep_r, brep_i,
                bt_r, bt_i, ct_r, ct_i, wcat_ref, ws_ref, a_ref, lag_scr):
    lr, li, dt_r = lamr_r[0], lami_r[0], jnp.exp(ldt_r[0])
    lrc, lic, dt_c = lamr_c[0], lami_c[0], jnp.exp(ldt_c[0])

    abar_r, abar_i = _powers(lr, li, dt_r, 1.0)
    num_r, num_i = abar_r - 1.0, abar_i
    den = lr * lr + li * li
    f_r = (num_r * lr + num_i * li) / den
    f_i = (num_i * lr - num_r * li) / den

    a_r, a_i = _powers(lr, li, dt_r, float(CHUNK))
    a_ref[0, 0:1, :] = a_r
    a_ref[0, 1:2, :] = a_i

    bb_r = f_r * brep_r[0] - f_i * brep_i[0]
    bb_i = f_r * brep_i[0] + f_i * brep_r[0]
    cb_r = crep_r[0] * bb_r - crep_i[0] * bb_i
    cb_i = crep_r[0] * bb_i + crep_i[0] * bb_r
    lag_lhs = jnp.concatenate([cb_r, -cb_i], axis=1)
    m_lane = lax.broadcasted_iota(jnp.int32, (DIRSTATE, CHUNK), 1)
    row_bwd = lax.broadcasted_iota(jnp.int32, (DIRSTATE, CHUNK), 0) >= STATE
    tab_r, tab_i = _powers(lrc, lic, dt_c, jnp.where(row_bwd, CHUNK - m_lane, m_lane).astype(F32))
    lag0 = jnp.where(m_lane == 0, 1.0, 0.0)
    lag_rhs = jnp.concatenate(
        [jnp.concatenate([jnp.where(row_bwd, lag0, tab_r), jnp.where(row_bwd, tab_r, 0.0)], axis=1),
         jnp.concatenate([jnp.where(row_bwd, 0.0, tab_i), jnp.where(row_bwd, tab_i, 0.0)], axis=1)],
        axis=0)
    lag = jnp.dot(lag_lhs, lag_rhs, precision=lax.Precision.HIGHEST,
                  preferred_element_type=F32).reshape(GROUP_CH, GROUP_CH, 2 * CHUNK)
    lag_bits = lax.bitcast_convert_type(lag.astype(MXU_DTYPE).astype(F32), jnp.uint32)
    half = GROUP_CH // 2
    lag_scr[...] = lag_bits[:, :half, :] | (lag_bits[:, half:, :] >> 16)

    s_idx = lax.broadcasted_iota(jnp.int32, (CHUNK, CHUNK), 0)
    m_idx = lax.broadcasted_iota(jnp.int32, (CHUNK, CHUNK), 1)
    use_fwd = s_idx + m_idx <= CHUNK - 1

    def toeplitz_rows(c_in, carry):
        row0 = pl.multiple_of(c_in * CHUNK, CHUNK)
        for pair in range(half):
            fwd = lag_scr[c_in, pair:pair + 1, 0:CHUNK]
            bwd = lag_scr[c_in, pair:pair + 1, CHUNK:2 * CHUNK]
            pre = jnp.where(use_fwd, jnp.broadcast_to(fwd, (CHUNK, CHUNK)),
                            jnp.broadcast_to(bwd, (CHUNK, CHUNK)))
            blk = pltpu.roll(pre, 0, 1, stride=1, stride_axis=0)
            unpacked = (blk & jnp.uint32(0xFFFF0000), blk << 16)
            for c_out, bits in zip((pair, pair + half), unpacked):
                wcat_ref[0, pl.ds(row0, CHUNK), c_out * CHUNK:(c_out + 1) * CHUNK] = (
                    lax.bitcast_convert_type(bits, F32).astype(wcat_ref.dtype))
        return carry

    lax.fori_loop(0, GROUP_CH, toeplitz_rows, 0, unroll=4)

    s_pos = lax.broadcasted_iota(jnp.int32, (CHUNK, DIRSTATE), 0)
    lane_bwd = lax.broadcasted_iota(jnp.int32, (CHUNK, DIRSTATE), 1) >= STATE
    k_in = jnp.where(lane_bwd, s_pos, CHUNK - 1 - s_pos).astype(F32)
    pin_r, pin_i = _powers(lr, li, dt_r, k_in)
    bbt_r = f_r * bt_r[0] - f_i * bt_i[0]
    bbt_i = f_r * bt_i[0] + f_i * bt_r[0]
    for c_in in range(GROUP_CH):
        b_r, b_i = bbt_r[c_in:c_in + 1, :], bbt_i[c_in:c_in + 1, :]
        rows = slice(c_in * CHUNK, (c_in + 1) * CHUNK)
        ws_ref[0, rows, 0:DIRSTATE] = (pin_r * b_r - pin_i * b_i).astype(ws_ref.dtype)
        ws_ref[0, rows, DIRSTATE:2 * DIRSTATE] = (pin_r * b_i + pin_i * b_r).astype(ws_ref.dtype)

    abar_cr, abar_ci = tab_r[:, 1:2], tab_i[:, 1:2]
    pout_r = jnp.where(row_bwd, tab_r, abar_cr * tab_r - abar_ci * tab_i)
    pout_i = jnp.where(row_bwd, tab_i, abar_cr * tab_i + abar_ci * tab_r)
    c_r_all, c_i_all = ct_r[0], ct_i[0]
    for c_out in range(GROUP_CH):
        c_r, c_i = c_r_all[:, c_out:c_out + 1], c_i_all[:, c_out:c_out + 1]
        cols = slice(c_out * CHUNK, (c_out + 1) * CHUNK)
        wcat_ref[0, TOEP:TOEP + DIRSTATE, cols] = (c_r * pout_r - c_i * pout_i).astype(wcat_ref.dtype)
        wcat_ref[0, TOEP + DIRSTATE:TOEP + 2 * DIRSTATE, cols] = (
            -(c_r * pout_i + c_i * pout_r)).astype(wcat_ref.dtype)


def _s5_operators(lam_re, lam_im, log_dt, b_re, b_im, c_re, c_im):
    depth = lam_re.shape[0]
    lg = depth * GROUPS

    def dirstate(a):
        return a.transpose(0, 2, 1, 3).reshape(lg, DIRSTATE)

    lamr, lami = dirstate(lam_re), dirstate(lam_im)
    ldt = dirstate(jnp.broadcast_to(log_dt[..., None], lam_re.shape))

    def b_layout(b):
        return b.transpose(0, 2, 4, 1, 3).reshape(depth, GROUPS, GROUP_CH, DIRSTATE)

    def c_layout(c):
        return c.transpose(0, 2, 3, 1, 4).reshape(depth, GROUPS, GROUP_CH, DIRSTATE)

    pair = (depth, GROUPS, GROUP_CH, GROUP_CH, DIRSTATE)

    def b_rep(b):
        return jnp.broadcast_to(b_layout(b)[:, :, :, None, :], pair).reshape(lg, GROUP_CH ** 2, DIRSTATE)

    def c_rep(c):
        return jnp.broadcast_to(c_layout(c)[:, :, None, :, :], pair).reshape(lg, GROUP_CH ** 2, DIRSTATE)

    def c_cols(c):
        return c.transpose(0, 2, 1, 4, 3).reshape(lg, DIRSTATE, GROUP_CH)

    args = (lamr[:, :, None], lami[:, :, None], ldt[:, :, None],
            lamr[:, None, :], lami[:, None, :], ldt[:, None, :],
            c_rep(c_re), c_rep(c_im), b_rep(b_re), b_rep(b_im),
            b_layout(b_re).reshape(lg, GROUP_CH, DIRSTATE), b_layout(b_im).reshape(lg, GROUP_CH, DIRSTATE),
            c_cols(c_re), c_cols(c_im))

    def spec(a):
        return pl.BlockSpec((1,) + a.shape[1:], lambda i: (i, 0, 0))

    return pl.pallas_call(
        _gen_kernel,
        grid=(lg,),
        in_specs=[spec(a) for a in args],
        out_specs=[pl.BlockSpec((1, TOEP + 2 * DIRSTATE, TOEP), lambda i: (i, 0, 0)),
                   pl.BlockSpec((1, TOEP, 2 * DIRSTATE), lambda i: (i, 0, 0)),
                   pl.BlockSpec((1, 2, DIRSTATE), lambda i: (i, 0, 0))],
        out_shape=[jax.ShapeDtypeStruct((lg, TOEP + 2 * DIRSTATE, TOEP), MXU_DTYPE),
                   jax.ShapeDtypeStruct((lg, TOEP, 2 * DIRSTATE), MXU_DTYPE),
                   jax.ShapeDtypeStruct((lg, 2, DIRSTATE), F32)],
        scratch_shapes=[pltpu.VMEM((GROUP_CH, GROUP_CH // 2, 2 * CHUNK), jnp.uint32)],
        compiler_params=_params(1),
        name="s5_operators",
    )(*args)


def _store_tile_major(ref, val, chunk0):
    for r in range(val.shape[0] // SUBLANES):
        for k in range(val.shape[1] // CHUNK):
            ref[r, (chunk0 + k) * SUBLANES:(chunk0 + k + 1) * SUBLANES, :] = (
                val[r * SUBLANES:(r + 1) * SUBLANES, k * CHUNK:(k + 1) * CHUNK])


def _load_tile_major(ref, chunk0, n_chunks):
    rows = [jnp.concatenate([ref[r, (chunk0 + k) * SUBLANES:(chunk0 + k + 1) * SUBLANES, :]
                             for k in range(n_chunks)], axis=1) for r in range(ref.shape[0])]
    return jnp.concatenate(rows, axis=0)


def _tile_major_spec(tm):
    return pl.BlockSpec((ROW_TILES, tm // CHUNK * SUBLANES, CHUNK), lambda i: (0, i, 0))


def _tile_major_shape(t):
    return jax.ShapeDtypeStruct((ROW_TILES, t // CHUNK * SUBLANES, CHUNK), F32)


def _mix_in_body(x, ng_ref, win_ref, lng_ref, lnb_ref, wst_ref, bs_ref, ya_ref, xb_ref, sz_ref):
    inv = lax.rsqrt(jnp.mean(x * x, axis=0, keepdims=True) + EPS)
    h = (x * inv * ng_ref[...]).astype(MXU_DTYPE)

    def proj(k):
        return jnp.dot(win_ref[k * WIDTH:(k + 1) * WIDTH, :], h, preferred_element_type=F32)

    p_v, p_xb, p_zb, p_u, p_za = proj(1), proj(3), proj(4), proj(0), proj(2)

    v = jax.nn.gelu(p_v)
    mu = jnp.mean(v, axis=0, keepdims=True)
    vc = v - mu
    var = jnp.mean(vc * vc, axis=0, keepdims=True)
    vn = (vc * lax.rsqrt(var + EPS) * lng_ref[...] + lnb_ref[...]).astype(MXU_DTYPE)

    _store_tile_major(xb_ref, p_xb, 0)
    sz_ref[...] = jax.nn.silu(p_zb)
    gated = jax.nn.gelu(p_u) * jax.nn.silu(p_za)

    n_chunks = x.shape[1] // CHUNK
    heads = []
    for pair in range(HEADS // 2):
        va = vn[(2 * pair) * HEAD_DIM:(2 * pair + 1) * HEAD_DIM, :]
        vb = vn[(2 * pair + 1) * HEAD_DIM:(2 * pair + 2) * HEAD_DIM, :]
        stacked = jnp.concatenate(
            [jnp.concatenate([va[:, k * CHUNK:(k + 1) * CHUNK], vb[:, k * CHUNK:(k + 1) * CHUNK]], axis=1)
             for k in range(n_chunks)], axis=0)
        mixed = jnp.dot(stacked, wst_ref[pair], preferred_element_type=F32) + bs_ref[pair]
        for half in range(2):
            heads.append(jnp.concatenate(
                [mixed[k * HEAD_DIM:(k + 1) * HEAD_DIM, half * CHUNK:(half + 1) * CHUNK]
                 for k in range(n_chunks)], axis=1))
    sv = jnp.concatenate(heads, axis=0)
    ya_ref[...] = (gated * sv).astype(ya_ref.dtype)


def _mix_in_kernel(x_ref, *refs):
    _mix_in_body(x_ref[...], *refs)


def _mix_in_natural_kernel(xin_ref, *refs, n_carried):
    weights, outs = refs[:6], refs[6 + n_carried:]
    ya_ref, xb_ref, sz_ref, xt_ref = outs
    x = jnp.concatenate([xin_ref[u].T for u in range(xin_ref.shape[0])], axis=1)
    xt_ref[...] = x
    _mix_in_body(x, *weights, ya_ref, xb_ref, sz_ref)


def _mix_in_out_shapes(t):
    return [jax.ShapeDtypeStruct((WIDTH, t), MXU_DTYPE),
            _tile_major_shape(t),
            jax.ShapeDtypeStruct((WIDTH, t), F32)]


def _mix_in(x_t, weights, tm):
    d, t = x_t.shape
    tile = lambda rows: pl.BlockSpec((rows, tm), lambda i: (0, i))
    return pl.pallas_call(
        _mix_in_kernel,
        grid=(t // tm,),
        in_specs=[tile(d)] + [_const_spec(w.shape) for w in weights],
        out_specs=[tile(WIDTH), _tile_major_spec(tm), tile(WIDTH)],
        out_shape=_mix_in_out_shapes(t),
        compiler_params=_params(1),
        name="mix_in",
    )(x_t, *weights)


def _natural_spec(n_seq, units, d):
    per_chunk = n_seq // units
    return pl.BlockSpec((units, CHUNK, d), lambda i: (i % per_chunk, i // per_chunk, 0))


def _mix_in_natural(x_nat, weights, tile0, t, units, carried):
    n_seq, l, d = x_nat.shape
    tm = units * CHUNK
    tile = lambda rows: pl.BlockSpec((rows, tm), lambda i: (0, tile0 + i))
    tile_major = pl.BlockSpec((ROW_TILES, units * SUBLANES, CHUNK), lambda i: (0, tile0 + i, 0))
    first_carried = 1 + len(weights)
    return pl.pallas_call(
        functools.partial(_mix_in_natural_kernel, n_carried=len(carried)),
        grid=(n_seq * l // tm,),
        in_specs=([_natural_spec(n_seq, units, d)] + [_const_spec(w.shape) for w in weights]
                  + [pl.BlockSpec(memory_space=pl.ANY)] * len(carried)),
        out_specs=[tile(WIDTH), tile_major, tile(WIDTH), tile(d)],
        out_shape=_mix_in_out_shapes(t) + [jax.ShapeDtypeStruct((d, t), F32)],
        input_output_aliases={first_carried + k: k for k in range(len(carried))},
        compiler_params=_params(1),
        name="mix_in_natural",
    )(x_nat, *weights, *carried)


def _chunk_scan(s_re, s_im, a_r, a_i, n_chunks, n_seq, is_fwd):
    h_r = jnp.zeros((n_seq, DIRSTATE), F32)
    h_i = jnp.zeros((n_seq, DIRSTATE), F32)
    fwd_r, fwd_i = [h_r] * n_chunks, [h_i] * n_chunks
    bwd_r, bwd_i = [h_r] * n_chunks, [h_i] * n_chunks
    rows = lambda a, j: a[j * n_seq:(j + 1) * n_seq, :]
    for i in range(n_chunks - 1):
        jf, jb = i, n_chunks - 1 - i
        in_r = jnp.where(is_fwd, rows(s_re, jf), rows(s_re, jb))
        in_i = jnp.where(is_fwd, rows(s_im, jf), rows(s_im, jb))
        h_r, h_i = a_r * h_r - a_i * h_i + in_r, a_r * h_i + a_i * h_r + in_i
        fwd_r[jf + 1], fwd_i[jf + 1] = h_r, h_i
        bwd_r[jb - 1], bwd_i[jb - 1] = h_r, h_i
    out_r = jnp.concatenate([jnp.where(is_fwd, f, b) for f, b in zip(fwd_r, bwd_r)], axis=0)
    out_i = jnp.concatenate([jnp.where(is_fwd, f, b) for f, b in zip(fwd_i, bwd_i)], axis=0)
    return out_r, out_i


def _s5_kernel(x_ref, wcat_ref, ws_ref, a_ref, o_ref, *, segments):
    n_rows = x_ref.shape[1] // SUBLANES

    def channel(c):
        return c // SUBLANES, pl.ds(c % SUBLANES, n_rows, stride=SUBLANES)

    def load(c):
        tile, rows = channel(c)
        return x_ref[tile, rows, :].astype(MXU_DTYPE)

    lhs = jnp.concatenate([load(c) for c in range(GROUP_CH)], axis=1)
    s = jnp.dot(lhs, ws_ref[0], preferred_element_type=F32)
    a_r, a_i = a_ref[0, 0:1, :], a_ref[0, 1:2, :]
    is_fwd = lax.broadcasted_iota(jnp.int32, (1, DIRSTATE), 1) < STATE
    parts_r, parts_i = [], []
    for row0, n_chunks, n_seq in segments:
        seg = s[row0:row0 + n_chunks * n_seq, :]
        h_r, h_i = _chunk_scan(seg[:, :DIRSTATE], seg[:, DIRSTATE:], a_r, a_i, n_chunks, n_seq, is_fwd)
        parts_r.append(h_r)
        parts_i.append(h_i)
    h = jnp.concatenate([jnp.concatenate(parts_r, axis=0), jnp.concatenate(parts_i, axis=0)], axis=1)
    lhs2 = jnp.concatenate([lhs, h.astype(lhs.dtype)], axis=1)
    for pair in range(GROUP_CH // 2):
        y = jnp.dot(lhs2, wcat_ref[0, :, pair * 2 * CHUNK:(pair + 1) * 2 * CHUNK],
                    preferred_element_type=F32)
        for half in range(2):
            tile, rows = channel(2 * pair + half)
            o_ref[tile, rows, :] = y[:, half * CHUNK:(half + 1) * CHUNK]


def _s5_mix(xb, wcat, ws, a_pow, layer, segments):
    group_tiles = GROUP_CH // SUBLANES
    block = pl.BlockSpec((group_tiles,) + xb.shape[1:], lambda i: (i, 0, 0))
    group = lambda i: (layer * GROUPS + i, 0, 0)
    return pl.pallas_call(
        functools.partial(_s5_kernel, segments=segments),
        grid=(GROUPS,),
        in_specs=[block,
                  pl.BlockSpec((1,) + wcat.shape[1:], group),
                  pl.BlockSpec((1,) + ws.shape[1:], group),
                  pl.BlockSpec((1,) + a_pow.shape[1:], group)],
        out_specs=block,
        out_shape=jax.ShapeDtypeStruct(xb.shape, F32),
        compiler_params=_params(1),
        name="s5_mix",
    )(xb, wcat, ws, a_pow)


def _mix_out_body(x_ref, ya_ref, xb_ref, ys_ref, sz_ref, dsk_ref, wg_ref, bg_ref, wo_ref):
    n_chunks = x_ref.shape[1] // CHUNK
    out = x_ref[...] + jnp.dot(wo_ref[:, :WIDTH], ya_ref[...], preferred_element_type=F32)
    y = _load_tile_major(ys_ref, 0, n_chunks) + dsk_ref[...] * _load_tile_major(xb_ref, 0, n_chunks)
    gy = jax.nn.gelu(y)
    gate = jax.nn.sigmoid(
        jnp.dot(wg_ref[...], gy.astype(MXU_DTYPE), preferred_element_type=F32) + bg_ref[...])
    yb = (gy * gate * sz_ref[...]).astype(MXU_DTYPE)
    return out + jnp.dot(wo_ref[:, WIDTH:], yb, preferred_element_type=F32)


def _mix_out_kernel(*refs):
    refs[-1][...] = _mix_out_body(*refs[:-1])


def _mix_out_final_kernel(*refs):
    fg_ref, o_ref = refs[-2:]
    out = _mix_out_body(*refs[:-2])
    out = out * lax.rsqrt(jnp.mean(out * out, axis=0, keepdims=True) + EPS) * fg_ref[...]
    for u in range(o_ref.shape[0]):
        o_ref[u] = out[:, u * CHUNK:(u + 1) * CHUNK].T


def _mix_out(x_t, ya, xb, ys, sz, weights, tm):
    d, t = x_t.shape
    tile = lambda rows: pl.BlockSpec((rows, tm), lambda i: (0, i))
    return pl.pallas_call(
        _mix_out_kernel,
        grid=(t // tm,),
        in_specs=([tile(d), tile(WIDTH), _tile_major_spec(tm), _tile_major_spec(tm), tile(WIDTH)]
                  + [_const_spec(w.shape) for w in weights]),
        out_specs=tile(d),
        out_shape=jax.ShapeDtypeStruct((d, t), F32),
        compiler_params=_params(1),
        name="mix_out",
    )(x_t, ya, xb, ys, sz, *weights)


def _mix_out_final(x_t, ya, xb, ys, sz, weights, tile0, units, n_seq, l):
    d = x_t.shape[0]
    tm = units * CHUNK
    tile = lambda rows: pl.BlockSpec((rows, tm), lambda i: (0, tile0 + i))
    tile_major = pl.BlockSpec((ROW_TILES, units * SUBLANES, CHUNK), lambda i: (0, tile0 + i, 0))
    return pl.pallas_call(
        _mix_out_final_kernel,
        grid=(n_seq * l // tm,),
        in_specs=([tile(d), tile(WIDTH), tile_major, tile_major, tile(WIDTH)]
                  + [_const_spec(w.shape) for w in weights]),
        out_specs=_natural_spec(n_seq, units, d),
        out_shape=jax.ShapeDtypeStruct((n_seq, l, d), F32),
        compiler_params=_params(1),
        name="mix_out_final",
    )(x_t, ya, xb, ys, sz, *weights)


def _units_per_tile(inputs):
    for units in (4, 2, 1):
        if all(x.shape[0] % units == 0 for x in inputs):
            return units


def _pair_block_diag(w_s_layer):
    wt = jnp.swapaxes(w_s_layer, 1, 2)
    zero = jnp.zeros_like(wt[0])
    return jnp.stack([jnp.block([[wt[2 * k], zero], [zero, wt[2 * k + 1]]])
                      for k in range(HEADS // 2)])


def kernel(x_prompt, x_sample, norm_g, w_in, ln_g, ln_b, w_s, b_s, lam_re, lam_im, log_dt,
           b_re, b_im, c_re, c_im, d_skip, w_glu, b_glu, w_out, final_g):
    depth = norm_g.shape[0]
    inputs = (x_prompt, x_sample)
    for x in inputs:
        assert x.shape[1] % CHUNK == 0 and x.shape[2] == D_MODEL
    units = _units_per_tile(inputs)
    tm = units * CHUNK
    t = sum(x.shape[0] * x.shape[1] for x in inputs)

    segments, tile0s, row0 = [], [], 0
    for x in inputs:
        n_seq, n_chunks = x.shape[0], x.shape[1] // CHUNK
        segments.append((row0, n_chunks, n_seq))
        tile0s.append(row0 // units)
        row0 += n_chunks * n_seq
    segments = tuple(segments)

    wcat, ws, a_pow = _s5_operators(lam_re, lam_im, log_dt, b_re, b_im, c_re, c_im)

    col = lambda a: a[:, None]
    x_t = None
    for i in range(depth):
        in_weights = (col(norm_g[i]), w_in[i].T.astype(MXU_DTYPE), col(ln_g[i]), col(ln_b[i]),
                      _pair_block_diag(w_s[i]).astype(MXU_DTYPE),
                      b_s[i].reshape(HEADS // 2, 1, 2 * CHUNK))
        out_weights = (col(d_skip[i]), w_glu[i].T.astype(MXU_DTYPE), col(b_glu[i]),
                       w_out[i].T.astype(MXU_DTYPE))
        if i == 0:
            carried = ()
            for x, tile0 in zip(inputs, tile0s):
                carried = _mix_in_natural(x, in_weights, tile0, t, units, carried)
            ya, xb, sz, x_t = carried
        else:
            ya, xb, sz = _mix_in(x_t, in_weights, tm)
        ys = _s5_mix(xb, wcat, ws, a_pow, i, segments)
        if i < depth - 1:
            x_t = _mix_out(x_t, ya, xb, ys, sz, out_weights, tm)
    return tuple(
        _mix_out_final(x_t, ya, xb, ys, sz, out_weights + (col(final_g),), tile0, units,
                       x.shape[0], x.shape[1])
        for x, tile0 in zip(inputs, tile0s))
```
